```python
import math
import jax, jax.numpy as jnp
from jax import lax
import numpy as np

D_MODEL = 1024
BATCH = 4
SEQ = 4096
DEPTH = 4
DEC_BATCH = 32
DEC_SEQ = 1
PAST_LEN = 8192
PAGE_SIZE = 128

D_HEAD = 64
MIX = D_MODEL
H_FOX = MIX // 2 // D_HEAD
H_DSA = MIX // 2 // D_HEAD
KV_DSA = 2
H_IDX = 8
D_IDX = 64
TOPK_DSA = 256
N_IN = 3 * H_FOX * D_HEAD + H_FOX + H_DSA * D_HEAD + 2 * KV_DSA * D_HEAD + H_IDX * D_IDX + D_IDX + H_IDX
PEER_HEADS = 8
N_KEYS = 128
N_EXPERTS = N_KEYS * N_KEYS
D_KEY = 128
PEER_TOPK = 16
Q_BLOCK = 128
PEER_CHUNK = 256
EPS = 1e-6

kernel_name = "hymba_fox_dsa_peer_step"


def _rms(x, g):
    xf = x.astype(jnp.float32)
    y = xf * lax.rsqrt(jnp.mean(xf * xf, axis=-1, keepdims=True) + EPS)
    return (y * g.astype(jnp.float32)).astype(x.dtype)


def _modulate(h, shift, scale):
    return h * (1 + scale) + shift


def _ada(c, w_ada, b_ada):
    mod = jnp.dot(jax.nn.silu(c), w_ada) + b_ada
    return jnp.split(mod[:, None, :], 6, axis=-1)


def _alibi_slopes(n):
    return jnp.exp2(-8.0 * jnp.arange(1, n + 1, dtype=jnp.float32) / n)


def _split_cols(z):
    sizes = (H_FOX * D_HEAD, H_FOX * D_HEAD, H_FOX * D_HEAD, H_FOX,
             H_DSA * D_HEAD, KV_DSA * D_HEAD, KV_DSA * D_HEAD,
             H_IDX * D_IDX, D_IDX, H_IDX)
    cuts = np.cumsum(sizes)[:-1].tolist()
    return jnp.split(z, cuts, axis=-1)


def _project(h, w_in, b_f, qn_f, kn_f, qn_d, kn_d):
    B, T, _ = h.shape
    z = jnp.dot(h, w_in)
    fq, fk, fv, ff, dq, dk, dv, iq, ik, iw = _split_cols(z)
    fq = _rms(fq.reshape(B, T, H_FOX, D_HEAD), qn_f)
    fk = _rms(fk.reshape(B, T, H_FOX, D_HEAD), kn_f)
    fv = fv.reshape(B, T, H_FOX, D_HEAD)
    logf = jax.nn.log_sigmoid(ff.astype(jnp.float32) + b_f.astype(jnp.float32))
    dq = _rms(dq.reshape(B, T, H_DSA, D_HEAD), qn_d)
    dk = _rms(dk.reshape(B, T, KV_DSA, D_HEAD), kn_d)
    dv = dv.reshape(B, T, KV_DSA, D_HEAD)
    iq = iq.reshape(B, T, H_IDX, D_IDX)
    return fq, fk, fv, logf, dq, dk, dv, iq, ik, iw


def _fox_attend(q, k, v, cq, ck, qpos, kpos):
    s = jnp.einsum("bqhd,bkhd->bhqk", q, k).astype(jnp.float32) * (D_HEAD ** -0.5)
    s = s + (jnp.transpose(cq, (0, 2, 1))[..., :, None] - jnp.transpose(ck, (0, 2, 1))[..., None, :])
    s = jnp.where(kpos[None, :] <= qpos[:, None], s, -jnp.inf)
    p = jax.nn.softmax(s, axis=-1).astype(v.dtype)
    return jnp.einsum("bhqk,bkhd->bqhd", p, v)


def _fox_prompt(q, k, v, logf):
    B, S, H, _ = q.shape
    csum = jnp.cumsum(logf, axis=1)
    kpos = jnp.arange(S)

    def blk(i):
        st = i * Q_BLOCK
        qb = lax.dynamic_slice_in_dim(q, st, Q_BLOCK, axis=1)
        cb = lax.dynamic_slice_in_dim(csum, st, Q_BLOCK, axis=1)
        return _fox_attend(qb, k, v, cb, csum, st + jnp.arange(Q_BLOCK), kpos)

    o = lax.map(blk, jnp.arange(S // Q_BLOCK))
    return jnp.transpose(o, (1, 0, 2, 3, 4)).reshape(B, S, H, D_HEAD)


def _index_scores(iq, iw, ik):
    r = jax.nn.relu(jnp.einsum("bqhi,bki->bqhk", iq, ik).astype(jnp.float32) * (D_IDX ** -0.5))
    return jnp.einsum("bqh,bqhk->bqk", iw.astype(jnp.float32) * (H_IDX ** -0.5), r)


def _select(scores, qpos, kpos, k_sel):
    sc = jnp.where(kpos[None, None, :] <= qpos[None, :, None], scores, -jnp.inf)
    _, idx = lax.top_k(sc, k_sel)
    ok = idx <= qpos[None, :, None]
    return idx, ok


def _dsa_attend(q, ksel, vsel, idx, ok, qpos, slopes):
    B, Tq, H, d = q.shape
    rep = H // KV_DSA
    qg = q.reshape(B, Tq, KV_DSA, rep, d)
    s = jnp.einsum("bqgrd,bqkgd->bqgrk", qg, ksel).astype(jnp.float32) * (d ** -0.5)
    dist = (qpos[None, :, None] - idx).astype(jnp.float32)
    s = s - slopes.reshape(KV_DSA, rep)[None, None, :, :, None] * dist[:, :, None, None, :]
    s = jnp.where(ok[:, :, None, None, :], s, -jnp.inf)
    p = jax.nn.softmax(s, axis=-1).astype(vsel.dtype)
    o = jnp.einsum("bqgrk,bqkgd->bqgrd", p, vsel)
    return o.reshape(B, Tq, H, d)


def _gather_rows(a, ix):
    return jax.vmap(lambda ab, ib: ab[ib])(a, ix)


def _dsa_prompt(dq, dk, dv, iq, ik, iw, slopes):
    B, S, H, _ = dq.shape
    k_sel = min(TOPK_DSA, S // 4)
    kpos = jnp.arange(S)
    kv = jnp.stack([dk, dv], axis=2)

    def blk(i):
        st = i * Q_BLOCK
        qpos = st + jnp.arange(Q_BLOCK)
        qb = lax.dynamic_slice_in_dim(dq, st, Q_BLOCK, axis=1)
        iqb = lax.dynamic_slice_in_dim(iq, st, Q_BLOCK, axis=1)
        iwb = lax.dynamic_slice_in_dim(iw, st, Q_BLOCK, axis=1)
        idx, ok = _select(_index_scores(iqb, iwb, ik), qpos, kpos, k_sel)
        sel = _gather_rows(kv, idx)
        return _dsa_attend(qb, sel[..., 0, :, :], sel[..., 1, :, :], idx, ok, qpos, slopes)

    o = lax.map(blk, jnp.arange(S // Q_BLOCK))
    return jnp.transpose(o, (1, 0, 2, 3, 4)).reshape(B, S, H, D_HEAD)


def _merge(o_f, o_d, w_out):
    B, T = o_f.shape[:2]
    o = jnp.concatenate([o_f.reshape(B, T, -1), o_d.reshape(B, T, -1)], axis=-1)
    return jnp.dot(o, w_out)


def _peer(h, wq, k1, k2, u, v):
    B, T, D = h.shape
    n = B * T
    ch = min(PEER_CHUNK, n)
    nch = -(-n // ch)
    xs = jnp.pad(h.reshape(n, D), ((0, nch * ch - n), (0, 0))).reshape(nch, ch, D)
    half = D_KEY // 2

    def chunk(xc):
        q = jnp.dot(xc, wq).reshape(ch, PEER_HEADS, D_KEY)
        s1 = jnp.einsum("thd,hnd->thn", q[..., :half], k1).astype(jnp.float32)
        s2 = jnp.einsum("thd,hnd->thn", q[..., half:], k2).astype(jnp.float32)
        v1, i1 = lax.top_k(s1, PEER_TOPK)
        v2, i2 = lax.top_k(s2, PEER_TOPK)
        cand = (v1[..., :, None] + v2[..., None, :]).reshape(ch, PEER_HEADS, PEER_TOPK * PEER_TOPK)
        cidx = (i1[..., :, None] * N_KEYS + i2[..., None, :]).reshape(ch, PEER_HEADS, PEER_TOPK * PEER_TOPK)
        best, pos = lax.top_k(cand, PEER_TOPK)
        eidx = jnp.take_along_axis(cidx, pos, axis=-1)
        g = jax.nn.softmax(best, axis=-1)
        act = jax.nn.gelu(jnp.einsum("td,thkd->thk", xc, u[eidx]).astype(jnp.float32), approximate=False)
        coef = (g * act).astype(xc.dtype)
        return jnp.einsum("thk,thkd->td", coef, v[eidx])

    y = lax.map(chunk, xs).reshape(nch * ch, D)[:n]
    return y.reshape(B, T, D)


def setup_inputs(seed: int = 0) -> dict:
    key = jax.random.key(seed)
    ks = jax.random.split(key, 32)
    f32 = jnp.float32
    n_pages = PAST_LEN // PAGE_SIZE
    n_used = DEC_BATCH * n_pages
    n_pool = n_used + max(1, n_used // 4)

    def nrm(k, shape, s):
        return jax.random.normal(k, shape, f32) * s

    page_table = jax.random.permutation(ks[9], n_pool)[:n_used].reshape(DEC_BATCH, n_pages).astype(jnp.int32)
    return {
        "x_prompt": nrm(ks[0], (BATCH, SEQ, D_MODEL), 1.0),
        "x_sample": nrm(ks[1], (DEC_BATCH, DEC_SEQ, D_MODEL), 1.0),
        "c_prompt": nrm(ks[2], (BATCH, D_MODEL), 1.0),
        "c_sample": nrm(ks[3], (DEC_BATCH, D_MODEL), 1.0),
        "cache_fox_kv": nrm(ks[4], (DEPTH, n_pool, PAGE_SIZE, 2, H_FOX, D_HEAD), 1.0),
        "cache_fox_logf": jax.nn.log_sigmoid(2.5 + nrm(ks[5], (DEPTH, n_pool, PAGE_SIZE, H_FOX), 0.5)),
        "cache_dsa_kv": nrm(ks[6], (DEPTH, n_pool, PAGE_SIZE, 2, KV_DSA, D_HEAD), 1.0),
        "cache_dsa_idx": nrm(ks[7], (DEPTH, n_pool, PAGE_SIZE, D_IDX), 1.0),
        "page_table": page_table,
        "w_ada": nrm(ks[10], (DEPTH, D_MODEL, 6 * D_MODEL), 0.2 * D_MODEL ** -0.5),
        "b_ada": nrm(ks[11], (DEPTH, 6 * D_MODEL), 0.02),
        "g_attn": 1.0 + nrm(ks[12], (DEPTH, D_MODEL), 0.05),
        "g_ffn": 1.0 + nrm(ks[13], (DEPTH, D_MODEL), 0.05),
        "w_in": nrm(ks[14], (DEPTH, D_MODEL, N_IN), D_MODEL ** -0.5),
        "b_f": 2.5 + nrm(ks[15], (DEPTH, H_FOX), 0.5),
        "qn_fox": 1.0 + nrm(ks[16], (DEPTH, D_HEAD), 0.05),
        "kn_fox": 1.0 + nrm(ks[17], (DEPTH, D_HEAD), 0.05),
        "qn_dsa": 1.0 + nrm(ks[18], (DEPTH, D_HEAD), 0.05),
        "kn_dsa": 1.0 + nrm(ks[19], (DEPTH, D_HEAD), 0.05),
        "w_out": nrm(ks[20], (DEPTH, MIX, D_MODEL), MIX ** -0.5),
        "peer_wq": nrm(ks[21], (DEPTH, D_MODEL, PEER_HEADS * D_KEY), D_MODEL ** -0.5),
        "peer_k1": nrm(ks[22], (DEPTH, PEER_HEADS, N_KEYS, D_KEY // 2), (D_KEY // 2) ** -0.5),
        "peer_k2": nrm(ks[23], (DEPTH, PEER_HEADS, N_KEYS, D_KEY // 2), (D_KEY // 2) ** -0.5),
        "peer_u": nrm(ks[24], (DEPTH, N_EXPERTS, D_MODEL), D_MODEL ** -0.5),
        "peer_v": nrm(ks[25], (DEPTH, N_EXPERTS, D_MODEL), PEER_HEADS ** -0.5),
    }


def reference(x_prompt, x_sample, c_prompt, c_sample, cache_fox_kv, cache_fox_logf, cache_dsa_kv,
              cache_dsa_idx, page_table, w_ada, b_ada, g_attn, g_ffn, w_in, b_f, qn_fox, kn_fox,
              qn_dsa, kn_dsa, w_out, peer_wq, peer_k1, peer_k2, peer_u, peer_v):
    Bd, T, _ = x_sample.shape
    n_pages = page_table.shape[1]
    P = n_pages * PAGE_SIZE
    L_s = P + T
    k_sel_s = min(TOPK_DSA, L_s // 4)
    slopes = _alibi_slopes(H_DSA)
    qpos_s = P + jnp.arange(T)
    kpos_s = jnp.arange(L_s)
    bidx = jnp.arange(Bd)[:, None, None]

    xp, xs = x_prompt, x_sample
    fkv_p, flf_p, dkv_p, dix_p = [], [], [], []
    fkv_s, flf_s, dkv_s, dix_s = [], [], [], []
    for l in range(DEPTH):
        mp = _ada(c_prompt, w_ada[l], b_ada[l])
        ms = _ada(c_sample, w_ada[l], b_ada[l])
        proj_w = (w_in[l], b_f[l], qn_fox[l], kn_fox[l], qn_dsa[l], kn_dsa[l])

        hp = _modulate(_rms(xp, g_attn[l]), mp[0], mp[1])
        fq, fk, fv, logf, dq, dk, dv, iq, ik, iw = _project(hp, *proj_w)
        o_f = _fox_prompt(fq, fk, fv, logf)
        o_d = _dsa_prompt(dq, dk, dv, iq, ik, iw, slopes)
        xp = xp + mp[2] * _merge(o_f, o_d, w_out[l])
        fkv_p.append(jnp.stack([fk, fv], axis=2))
        flf_p.append(logf)
        dkv_p.append(jnp.stack([dk, dv], axis=2))
        dix_p.append(ik)

        hs = _modulate(_rms(xs, g_attn[l]), ms[0], ms[1])
        fq, fk, fv, logf, dq, dk, dv, iq, ik, iw = _project(hs, *proj_w)
        fkv_past = cache_fox_kv[l, page_table].reshape(Bd, P, 2, H_FOX, D_HEAD)
        flf_past = cache_fox_logf[l, page_table].reshape(Bd, P, H_FOX).astype(jnp.float32)
        k_all = jnp.concatenate([fkv_past[:, :, 0], fk], axis=1)
        v_all = jnp.concatenate([fkv_past[:, :, 1], fv], axis=1)
        csum = jnp.cumsum(jnp.concatenate([flf_past, logf], axis=1), axis=1)
        o_f = _fox_attend(fq, k_all, v_all, csum[:, P:], csum, qpos_s, kpos_s)

        ik_all = jnp.concatenate([cache_dsa_idx[l, page_table].reshape(Bd, P, D_IDX), ik], axis=1)
        idx, ok = _select(_index_scores(iq, iw, ik_all), qpos_s, kpos_s, k_sel_s)
        pidx = jnp.minimum(idx, P - 1)
        phys = page_table[bidx, pidx // PAGE_SIZE]
        past_sel = cache_dsa_kv[l, phys, pidx % PAGE_SIZE]
        kv_new = jnp.stack([dk, dv], axis=2)
        new_sel = _gather_rows(kv_new, jnp.clip(idx - P, 0, T - 1))
        sel = jnp.where((idx < P)[..., None, None, None], past_sel, new_sel)
        o_d = _dsa_attend(dq, sel[..., 0, :, :], sel[..., 1, :, :], idx, ok, qpos_s, slopes)
        xs = xs + ms[2] * _merge(o_f, o_d, w_out[l])
        fkv_s.append(jnp.stack([fk, fv], axis=2))
        flf_s.append(logf)
        dkv_s.append(kv_new)
        dix_s.append(ik)

        peer_w = (peer_wq[l], peer_k1[l], peer_k2[l], peer_u[l], peer_v[l])
        xp = xp + mp[5] * _peer(_modulate(_rms(xp, g_ffn[l]), mp[3], mp[4]), *peer_w)
        xs = xs + ms[5] * _peer(_modulate(_rms(xs, g_ffn[l]), ms[3], ms[4]), *peer_w)

    fox_kv_p = jnp.stack(fkv_p)
    fox_logf_p = jnp.stack(flf_p)
    dsa_kv_p = jnp.stack(dkv_p)
    dsa_idx_p = jnp.stack(dix_p)
    fox_kv_s = jnp.stack(fkv_s)
    fox_logf_s = jnp.stack(flf_s)
    dsa_kv_s = jnp.stack(dkv_s)
    dsa_idx_s = jnp.stack(dix_s)
    return (xp, xs, fox_kv_p, fox_logf_p, dsa_kv_p, dsa_idx_p, fox_kv_s, fox_logf_s, dsa_kv_s, dsa_idx_s)
```

```python
import functools
import math

import jax
import jax.numpy as jnp
import numpy as np
from jax import lax
from jax.experimental import pallas as pl
from jax.experimental.pallas import tpu as pltpu

D_HEAD = 64
H_FOX = 8
H_DSA = 8
KV_DSA = 2
H_IDX = 8
D_IDX = 64
TOPK_DSA = 256
PEER_HEADS = 8
N_KEYS = 128
D_KEY = 128
PEER_TOPK = 16
EPS = 1e-6

LANES = 128
F32 = jnp.float32
CD = jnp.bfloat16
VMEM_LIMIT = 48 * 1024 * 1024

W_FOX = H_FOX * D_HEAD
W_DSA = H_DSA * D_HEAD
W_DKV = KV_DSA * D_HEAD
W_IQ = H_IDX * D_IDX
C_FQ, C_FK, C_FV, C_DQ, C_IQ = 0, 512, 1024, 1536, 2048
C_DK, C_DV, C_MISC, N_PERM = 2560, 2688, 2816, 2944
M_IK, M_FF, M_IW = 0, 64, 72

NEG_INF = float("-inf")
INT_MIN = -2147483648


def _cparams(sem):
    return pltpu.CompilerParams(dimension_semantics=sem, vmem_limit_bytes=VMEM_LIMIT)


def _dot(a, b):
    return jnp.dot(a, b, preferred_element_type=F32)


def _dot_nt(a, b):
    return lax.dot_general(a, b, (((1,), (1,)), ((), ())), preferred_element_type=F32)


def _dot_tn(a, b):
    return lax.dot_general(a, b, (((0,), (0,)), ((), ())), preferred_element_type=F32)


def _dot_split(x, m, n_split, dot=_dot):
    acc = None
    r = x
    for i in range(n_split):
        p = r.astype(CD)
        d = dot(p, m)
        acc = d if acc is None else acc + d
        if i + 1 < n_split:
            r = r - p.astype(F32)
    return acc


def _dot_split_rhs(m, x, n_split):
    acc = None
    r = x
    for i in range(n_split):
        p = r.astype(CD)
        d = _dot(m, p)
        acc = d if acc is None else acc + d
        if i + 1 < n_split:
            r = r - p.astype(F32)
    return acc


def _rms_mod(x, g, shift, scale):
    ms = jnp.mean(x * x, axis=-1, keepdims=True)
    y = x * lax.rsqrt(ms + EPS) * g
    return y * (1.0 + scale) + shift


def _ada_kernel(c_ref, w_ref, b_ref, o_ref):
    c = c_ref[...]
    a = (c * jax.nn.sigmoid(c)).astype(CD)
    o_ref[...] = _dot(a, w_ref[...].astype(CD)) + b_ref[...]


def _ada(c_all, w_ada, b_ada):
    depth, d, n6 = w_ada.shape
    r = c_all.shape[0]
    tn = n6 // 4
    return pl.pallas_call(
        _ada_kernel,
        grid=(depth, n6 // tn),
        in_specs=[
            pl.BlockSpec((r, d), lambda l, j: (0, 0)),
            pl.BlockSpec((None, d, tn), lambda l, j: (l, 0, j)),
            pl.BlockSpec((None, 1, tn), lambda l, j: (l, 0, j)),
        ],
        out_specs=pl.BlockSpec((None, r, tn), lambda l, j: (l, 0, j)),
        out_shape=jax.ShapeDtypeStruct((depth, r, n6), F32),
        compiler_params=_cparams(("arbitrary", "arbitrary")),
        name="ada",
    )(c_all, w_ada, b_ada.reshape(depth, 1, n6))


def _head_norm(z, gain, seg):
    sq = z * z
    ss = _dot_split(sq, seg, 2)
    return z * lax.rsqrt(ss * (1.0 / D_HEAD) + EPS) * gain


def _proj_kernel(x_ref, sh_ref, sc_ref, g_ref, w_ref, bias_ref, gains_ref, seg_ref,
                 fkv_ref, dkv_ref, misc_ref, att_ref):
    h = _rms_mod(x_ref[...], g_ref[...], sh_ref[...], sc_ref[...]).astype(CD)
    seg = seg_ref[...]

    def seg_dot(c0, width):
        return _dot(h, w_ref[:, c0:c0 + width])

    fq = _head_norm(seg_dot(C_FQ, W_FOX), gains_ref[0:1, :], seg)
    att_ref[:, C_FQ:C_FQ + W_FOX] = fq.astype(CD)
    fk = _head_norm(seg_dot(C_FK, W_FOX), gains_ref[1:2, :], seg)
    fkv_ref[:, 0:W_FOX] = fk
    att_ref[:, C_FK:C_FK + W_FOX] = fk.astype(CD)
    fv = seg_dot(C_FV, W_FOX)
    fkv_ref[:, W_FOX:2 * W_FOX] = fv
    att_ref[:, C_FV:C_FV + W_FOX] = fv.astype(CD)
    dq = _head_norm(seg_dot(C_DQ, W_DSA), gains_ref[2:3, :], seg)
    att_ref[:, C_DQ:C_DQ + W_DSA] = dq.astype(CD)
    att_ref[:, C_IQ:C_IQ + W_IQ] = seg_dot(C_IQ, W_IQ).astype(CD)
    dk = _head_norm(seg_dot(C_DK, W_DKV), gains_ref[3:4, 0:W_DKV], seg[0:W_DKV, 0:W_DKV])
    dkv_ref[:, 0:W_DKV] = dk
    att_ref[:, C_DK:C_DK + W_DKV] = dk.astype(CD)
    dv = seg_dot(C_DV, W_DKV)
    dkv_ref[:, W_DKV:2 * W_DKV] = dv
    att_ref[:, C_DV:C_DV + W_DKV] = dv.astype(CD)
    zm = seg_dot(C_MISC, LANES)
    zb = zm + bias_ref[...]
    logsig = jnp.minimum(zb, 0.0) - jnp.log1p(jnp.exp(-jnp.abs(zb)))
    lane = lax.broadcasted_iota(jnp.int32, zm.shape, 1)
    misc = jnp.where((lane >= M_FF) & (lane < M_FF + H_FOX), logsig, zm)
    misc_ref[...] = misc
    att_ref[:, C_MISC:C_MISC + LANES] = misc.astype(CD)


def _mod_spec(mod, tm, d):
    if mod.shape[1] == 1:
        return pl.BlockSpec((None, 1, d), lambda b, i: (b, 0, 0))
    return pl.BlockSpec((None, tm, d), lambda b, i: (b, i, 0))


def _proj(x, shift, scale, g, w_perm, bias_row, gains, seg, tm):
    bsz, t, d = x.shape
    row = lambda width: pl.BlockSpec((None, tm, width), lambda b, i: (b, i, 0))
    const = lambda shape: pl.BlockSpec(shape, lambda b, i: (0, 0))
    return pl.pallas_call(
        _proj_kernel,
        grid=(bsz, t // tm),
        in_specs=[row(d), _mod_spec(shift, tm, d), _mod_spec(scale, tm, d), const((1, d)),
                  const((d, N_PERM)), const((1, LANES)), const((4, W_FOX)), const((W_FOX, W_FOX))],
        out_specs=[row(2 * W_FOX), row(2 * W_DKV), row(LANES), row(N_PERM)],
        out_shape=[jax.ShapeDtypeStruct((bsz, t, 2 * W_FOX), F32),
                   jax.ShapeDtypeStruct((bsz, t, 2 * W_DKV), F32),
                   jax.ShapeDtypeStruct((bsz, t, LANES), F32),
                   jax.ShapeDtypeStruct((bsz, t, N_PERM), CD)],
        compiler_params=_cparams(("arbitrary", "arbitrary")),
        name="proj",
    )(x, shift, scale, g, w_perm, bias_row, gains, seg)


def _cumsum_kernel(x_ref, u_ref, o_ref):
    t = x_ref.shape[-1]
    u = u_ref[...]
    carry = jnp.zeros((x_ref.shape[0], 1), F32)
    for j in range(t // LANES):
        c = _dot_split(x_ref[:, j * LANES:(j + 1) * LANES], u, 3) + carry
        o_ref[:, j * LANES:(j + 1) * LANES] = c
        carry = c[:, LANES - 1:LANES]


def _cumsum(lf_t, u_incl):
    bsz, h, t = lf_t.shape
    return pl.pallas_call(
        _cumsum_kernel,
        grid=(bsz,),
        in_specs=[pl.BlockSpec((None, h, t), lambda b: (b, 0, 0)),
                  pl.BlockSpec((LANES, LANES), lambda b: (0, 0))],
        out_specs=pl.BlockSpec((None, h, t), lambda b: (b, 0, 0)),
        out_shape=jax.ShapeDtypeStruct((bsz, h, t), F32),
        compiler_params=_cparams(("arbitrary",)),
        name="cumsum",
    )(lf_t, u_incl)


def _fox_kernel(q_ref, k_ref, v_ref, cq_ref, ck_ref, o_ref, m_scr, l_scr, acc_scr, *, tq, tk):
    qi = pl.program_id(2)
    ki = pl.program_id(3)

    @pl.when(ki == 0)
    def _():
        m_scr[...] = jnp.full(m_scr.shape, NEG_INF, F32)
        l_scr[...] = jnp.zeros(l_scr.shape, F32)
        acc_scr[...] = jnp.zeros(acc_scr.shape, F32)

    @pl.when(ki <= qi)
    def _():
        s = _dot_nt(q_ref[...], k_ref[...]) * (D_HEAD ** -0.5)
        s = s + (cq_ref[...] - ck_ref[...])
        qpos = qi * tq + lax.broadcasted_iota(jnp.int32, s.shape, 0)
        kpos = ki * tk + lax.broadcasted_iota(jnp.int32, s.shape, 1)
        s = jnp.where(kpos <= qpos, s, NEG_INF)
        m_prev = m_scr[...]
        m_new = jnp.maximum(m_prev, jnp.max(s, axis=1, keepdims=True))
        alpha = jnp.exp(m_prev - m_new)
        p = jnp.exp(s - m_new)
        l_scr[...] = alpha * l_scr[...] + jnp.sum(p, axis=1, keepdims=True)
        acc_scr[...] = alpha * acc_scr[...] + _dot(p.astype(CD), v_ref[...])
        m_scr[...] = m_new

    @pl.when(ki == pl.num_programs(3) - 1)
    def _():
        o_ref[...] = acc_scr[...] / l_scr[...]


def _fox_prompt(q, k, v, cq, ck, tq):
    bsz, h, t, dh = q.shape
    tk = tq
    nq = t // tq
    qspec = pl.BlockSpec((None, None, tq, dh), lambda b, hh, i, j: (b, hh, i, 0))
    kspec = pl.BlockSpec((None, None, tk, dh), lambda b, hh, i, j: (b, hh, jnp.minimum(i, j), 0))
    return pl.pallas_call(
        functools.partial(_fox_kernel, tq=tq, tk=tk),
        grid=(bsz, h, nq, nq),
        in_specs=[qspec, kspec, kspec,
                  pl.BlockSpec((None, None, tq, 1), lambda b, hh, i, j: (b, hh, i, 0)),
                  pl.BlockSpec((None, None, 1, tk), lambda b, hh, i, j: (b, hh, 0, jnp.minimum(i, j)))],
        out_specs=qspec,
        out_shape=jax.ShapeDtypeStruct((bsz, h, t, dh), F32),
        scratch_shapes=[pltpu.VMEM((tq, 1), F32), pltpu.VMEM((tq, 1), F32), pltpu.VMEM((tq, dh), F32)],
        compiler_params=_cparams(("arbitrary",) * 4),
        name="fox_prompt",
    )(q, k, v, cq, ck)


def _sort_key(x):
    b = pltpu.bitcast(x, jnp.int32)
    return b ^ (lax.shift_right_arithmetic(b, 31) & jnp.int32(0x7FFFFFFF))


def _kth_largest_key(count_ge, shape, k):
    def body(i, res):
        cand = res + lax.shift_left(jnp.int32(1), jnp.int32(31) - i)
        return jnp.where(count_ge(cand) >= k, cand, res)

    return lax.fori_loop(0, 32, body, jnp.full(shape, INT_MIN, jnp.int32))


def _dsa_kernel(iq_ref, iw_ref, ik_ref, dq_ref, dk_ref, dv_ref, us_ref, o_ref, *, tq, k_sel):
    qi = pl.program_id(1)
    t = ik_ref.shape[0]
    ik = ik_ref[...]
    w = iw_ref[...] * (H_IDX ** -0.5)
    score = jnp.zeros((tq, t), F32)
    for h in range(H_IDX):
        r = jnp.maximum(_dot_nt(iq_ref[h], ik) * (D_IDX ** -0.5), 0.0)
        score = score + w[:, h:h + 1] * r
    qpos = qi * tq + lax.broadcasted_iota(jnp.int32, (tq, t), 0)
    kpos = lax.broadcasted_iota(jnp.int32, (tq, t), 1)
    causal = kpos <= qpos
    key = _sort_key(jnp.where(causal, score, NEG_INF))

    def count_ge(cand):
        return jnp.sum(jnp.where(key >= cand, 1.0, 0.0), axis=1, keepdims=True)

    thr = _kth_largest_key(count_ge, (tq, 1), float(k_sel))
    gt = key > thr
    eq = (key == thr) & causal
    n_gt = jnp.sum(jnp.where(gt, 1.0, 0.0), axis=1, keepdims=True)
    n_eq = jnp.sum(jnp.where(eq, 1.0, 0.0), axis=1, keepdims=True)
    need = float(k_sel) - n_gt
    sel_scr = jnp.where((gt | eq) & causal, 1.0, 0.0)

    def tie_break(sel):
        eqf = jnp.where(eq, 1.0, 0.0).astype(CD)
        us = us_ref[...]
        carry = jnp.zeros((tq, 1), F32)
        pieces = []
        for c in range(t // LANES):
            blk = eqf[:, c * LANES:(c + 1) * LANES]
            rank = _dot(blk, us) + carry
            pieces.append(rank)
            carry = carry + jnp.sum(blk.astype(F32), axis=1, keepdims=True)
        rank = jnp.concatenate(pieces, axis=1)
        keep = (gt & causal) | (eq & (rank < need))
        return jnp.where(keep, 1.0, 0.0)

    excess = jnp.max(n_eq - need) > 0.0
    sel_f = lax.cond(excess, tie_break, lambda s: s, sel_scr)
    sel = sel_f > 0.0

    dist = (qpos - kpos).astype(F32)
    rep = H_DSA // KV_DSA
    for h in range(H_DSA):
        g = h // rep
        slope = 2.0 ** (-8.0 * (h + 1) / H_DSA)
        s = _dot_nt(dq_ref[h], dk_ref[g]) * (D_HEAD ** -0.5) - slope * dist
        s = jnp.where(sel, s, NEG_INF)
        m = jnp.max(s, axis=1, keepdims=True)
        p = jnp.exp(s - m)
        l = jnp.sum(p, axis=1, keepdims=True)
        o_ref[h] = _dot(p.astype(CD), dv_ref[g]) / l


def _dsa_prompt(iq, iw, ik, dq, dk, dv, u_strict, tq):
    bsz, hi, t, di = iq.shape
    k_sel = min(TOPK_DSA, t // 4)
    qh = lambda nh: pl.BlockSpec((None, nh, tq, D_HEAD), lambda b, i: (b, 0, i, 0))
    full = lambda nh: pl.BlockSpec((None, nh, t, D_HEAD), lambda b, i: (b, 0, 0, 0))
    return pl.pallas_call(
        functools.partial(_dsa_kernel, tq=tq, k_sel=k_sel),
        grid=(bsz, t // tq),
        in_specs=[qh(hi),
                  pl.BlockSpec((None, tq, H_IDX), lambda b, i: (b, i, 0)),
                  pl.BlockSpec((None, t, di), lambda b, i: (b, 0, 0)),
                  qh(H_DSA), full(KV_DSA), full(KV_DSA),
                  pl.BlockSpec((LANES, LANES), lambda b, i: (0, 0))],
        out_specs=qh(H_DSA),
        out_shape=jax.ShapeDtypeStruct((bsz, H_DSA, t, D_HEAD), F32),
        compiler_params=_cparams(("arbitrary", "arbitrary")),
        name="dsa_prompt",
    )(iq, iw, ik, dq, dk, dv, u_strict)


def _merge_kernel(of_ref, od_ref, w_ref, gate_ref, x_ref, o_ref):
    half = of_ref.shape[-1]
    y = _dot(of_ref[...].astype(CD), w_ref[0:half, :]) + _dot(od_ref[...].astype(CD), w_ref[half:, :])
    o_ref[...] = x_ref[...] + gate_ref[...] * y


def _merge(o_f, o_d, w_out, gate, x, tm):
    bsz, t, d = x.shape
    half = o_f.shape[-1]
    row = lambda width: pl.BlockSpec((None, tm, width), lambda b, i: (b, i, 0))
    return pl.pallas_call(
        _merge_kernel,
        grid=(bsz, t // tm),
        in_specs=[row(half), row(half), pl.BlockSpec((2 * half, d), lambda b, i: (0, 0)),
                  _mod_spec(gate, tm, d), row(d)],
        out_specs=row(d),
        out_shape=jax.ShapeDtypeStruct((bsz, t, d), F32),
        compiler_params=_cparams(("arbitrary", "arbitrary")),
        name="merge",
    )(o_f, o_d, w_out, gate, x)


def _top_rows(x, n):
    vals = []
    for _ in range(n):
        m = jnp.max(x, axis=0, keepdims=True)
        vals.append(m)
        x = jnp.where(x == m, NEG_INF, x)
    return vals


_PEER_PAIRS = [(i, j) for i in range(PEER_TOPK) for j in range(PEER_TOPK) if (i + 1) * (j + 1) <= PEER_TOPK]
_PEER_CAND_ROWS = -(-len(_PEER_PAIRS) // 8) * 8


def _peer_a_kernel(x_ref, sh_ref, sc_ref, g_ref, wqt_ref, kbdt_ref, hq_ref, st_ref, et_ref, tau_ref,
                   cand_scr, candw_scr):
    hq = _rms_mod(x_ref[...], g_ref[...], sh_ref[...], sc_ref[...]).astype(CD)
    hq_ref[...] = hq
    qt = _dot_nt(wqt_ref[...], hq)
    st = _dot(kbdt_ref[...], qt.astype(CD))
    st_ref[...] = st
    cand_scr[...] = jnp.full(cand_scr.shape, NEG_INF, F32)
    candw_scr[...] = jnp.zeros(candw_scr.shape, F32)
    for h in range(PEER_HEADS):
        r0 = h * 2 * N_KEYS
        s1 = st[r0:r0 + N_KEYS]
        s2 = st[r0 + N_KEYS:r0 + 2 * N_KEYS]
        v1 = _top_rows(s1, PEER_TOPK)
        v2 = _top_rows(s2, PEER_TOPK)
        w1 = [jnp.exp(a - v1[0]) for a in v1]
        w2 = [jnp.exp(a - v2[0]) for a in v2]
        for r, (i, j) in enumerate(_PEER_PAIRS):
            cand_scr[r:r + 1, :] = v1[i] + v2[j]
            candw_scr[r:r + 1, :] = w1[i] * w2[j]
        cand = cand_scr[...]
        candw = candw_scr[...]
        tau = _top_rows(cand, PEER_TOPK)[-1]
        z = jnp.sum(jnp.where(cand >= tau, candw, 0.0), axis=0, keepdims=True)
        et_ref[r0:r0 + N_KEYS, :] = jnp.exp(s1 - v1[0])
        et_ref[r0 + N_KEYS:r0 + 2 * N_KEYS, :] = jnp.exp(s2 - v2[0]) / z
        tau_ref[h:h + 1, :] = tau


def _peer_a(x, shift, scale, g, wq_t, kbd_t, tm):
    bsz, t, d = x.shape
    ns = PEER_HEADS * 2 * N_KEYS
    row = pl.BlockSpec((None, tm, d), lambda b, i: (b, i, 0))
    col = lambda rows: pl.BlockSpec((None, rows, tm), lambda b, i: (b, 0, i))
    const = lambda shape: pl.BlockSpec(shape, lambda b, i: (0, 0))
    return pl.pallas_call(
        _peer_a_kernel,
        grid=(bsz, t // tm),
        in_specs=[row, _mod_spec(shift, tm, d), _mod_spec(scale, tm, d), const((1, d)),
                  const(wq_t.shape), const(kbd_t.shape)],
        out_specs=[row, col(ns), col(ns), col(PEER_HEADS)],
        out_shape=[jax.ShapeDtypeStruct((bsz, t, d), CD),
                   jax.ShapeDtypeStruct((bsz, ns, t), F32),
                   jax.ShapeDtypeStruct((bsz, ns, t), F32),
                   jax.ShapeDtypeStruct((bsz, PEER_HEADS, t), F32)],
        scratch_shapes=[pltpu.VMEM((_PEER_CAND_ROWS, tm), F32), pltpu.VMEM((_PEER_CAND_ROWS, tm), F32)],
        compiler_params=_cparams(("arbitrary", "arbitrary")),
        name="peer_a",
    )(x, shift, scale, g, wq_t, kbd_t)


def _peer_b_kernel(hq_ref, st_ref, et_ref, tau_ref, u_ref, v_ref, x_ref, gate_ref, o_ref, acc_scr, *, ic):
    c = pl.program_id(2)

    @pl.when(c == 0)
    def _():
        acc_scr[...] = jnp.zeros(acc_scr.shape, F32)

    act = _dot_nt(u_ref[...], hq_ref[...])
    gel = 0.5 * act * (1.0 + lax.erf(act * (2.0 ** -0.5)))
    coef = []
    for ii in range(ic):
        i1 = c * ic + ii
        gate = jnp.zeros((N_KEYS, act.shape[1]), F32)
        for h in range(PEER_HEADS):
            r0 = h * 2 * N_KEYS
            s1 = st_ref[pl.ds(r0 + i1, 1), :]
            e1 = et_ref[pl.ds(r0 + i1, 1), :]
            s2 = st_ref[r0 + N_KEYS:r0 + 2 * N_KEYS, :]
            e2 = et_ref[r0 + N_KEYS:r0 + 2 * N_KEYS, :]
            gate = gate + jnp.where(s1 + s2 >= tau_ref[h:h + 1, :], e1 * e2, 0.0)
        coef.append((gate * gel[ii * N_KEYS:(ii + 1) * N_KEYS]).astype(CD))
    coef = jnp.concatenate(coef, axis=0)
    acc_scr[...] += _dot_tn(coef, v_ref[...])

    @pl.when(c == pl.num_programs(2) - 1)
    def _():
        o_ref[...] = x_ref[...] + gate_ref[...] * acc_scr[...]


def _peer_b(hq, st, et, tau, u_c, v_c, x, gate, tm, ic):
    bsz, t, d = x.shape
    ns = st.shape[1]
    ec = ic * N_KEYS
    row = pl.BlockSpec((None, tm, d), lambda b, i, c: (b, i, 0))
    col = lambda rows: pl.BlockSpec((None, rows, tm), lambda b, i, c: (b, 0, i))
    wspec = pl.BlockSpec((ec, d), lambda b, i, c: (c, 0))
    if gate.shape[1] == 1:
        gspec = pl.BlockSpec((None, 1, d), lambda b, i, c: (b, 0, 0))
    else:
        gspec = row
    return pl.pallas_call(
        functools.partial(_peer_b_kernel, ic=ic),
        grid=(bsz, t // tm, N_KEYS // ic),
        in_specs=[row, col(ns), col(ns), col(PEER_HEADS), wspec, wspec, row, gspec],
        out_specs=row,
        out_shape=jax.ShapeDtypeStruct((bsz, t, d), F32),
        scratch_shapes=[pltpu.VMEM((tm, d), F32)],
        compiler_params=_cparams(("arbitrary",) * 3),
        name="peer_b",
    )(hq, st, et, tau, u_c, v_c, x, gate)


def _fox_dec_kernel(pt_ref, qm_ref, kvn_ref, lfn_ref, kv_ref, lft_ref, msuf_ref, o_ref,
                    m_scr, l_scr, acc_scr, carry_scr):
    j = pl.program_id(1)
    qm = qm_ref[...]
    scale = D_HEAD ** -0.5

    @pl.when(j == 0)
    def _():
        kn = kvn_ref[:, 0:W_FOX]
        m_scr[...] = jnp.sum(qm.astype(F32) * kn.astype(CD).astype(F32), axis=1, keepdims=True) * scale
        l_scr[...] = jnp.ones(l_scr.shape, F32)
        acc_scr[...] = jnp.broadcast_to(kvn_ref[:, W_FOX:2 * W_FOX].astype(CD).astype(F32), acc_scr.shape)
        carry_scr[...] = lfn_ref[...]

    kp = kv_ref[:, 0:W_FOX].astype(CD)
    vp = kv_ref[:, W_FOX:2 * W_FOX].astype(CD)
    lft = lft_ref[...]
    bias = _dot_split(lft, msuf_ref[...], 3) + carry_scr[...]
    s = _dot_nt(qm, kp) * scale + bias
    m_prev = m_scr[...]
    m_new = jnp.maximum(m_prev, jnp.max(s, axis=1, keepdims=True))
    alpha = jnp.exp(m_prev - m_new)
    p = jnp.exp(s - m_new)
    l_scr[...] = alpha * l_scr[...] + jnp.sum(p, axis=1, keepdims=True)
    acc_scr[...] = alpha * acc_scr[...] + _dot(p.astype(CD), vp)
    m_scr[...] = m_new
    carry_scr[...] = carry_scr[...] + jnp.sum(lft, axis=1, keepdims=True)

    @pl.when(j == pl.num_programs(1) - 1)
    def _():
        o_ref[...] = acc_scr[...] / l_scr[...]


def _fox_decode(page_table, qm, kv_new, lf_new_t, cache_kv, cache_lf_t, m_suf, layer):
    bd, n_pages = page_table.shape
    page = cache_kv.shape[2]
    grid_spec = pltpu.PrefetchScalarGridSpec(
        num_scalar_prefetch=1,
        grid=(bd, n_pages),
        in_specs=[
            pl.BlockSpec((None, H_FOX, W_FOX), lambda b, j, pt: (b, 0, 0)),
            pl.BlockSpec((None, 1, 2 * W_FOX), lambda b, j, pt: (b, 0, 0)),
            pl.BlockSpec((None, H_FOX, 1), lambda b, j, pt: (b, 0, 0)),
            pl.BlockSpec((None, None, page, 2 * W_FOX),
                         lambda b, j, pt: (layer, pt[b, n_pages - 1 - j], 0, 0)),
            pl.BlockSpec((None, None, H_FOX, page),
                         lambda b, j, pt: (layer, pt[b, n_pages - 1 - j], 0, 0)),
            pl.BlockSpec((page, page), lambda b, j, pt: (0, 0)),
        ],
        out_specs=pl.BlockSpec((None, H_FOX, W_FOX), lambda b, j, pt: (b, 0, 0)),
        scratch_shapes=[pltpu.VMEM((H_FOX, 1), F32), pltpu.VMEM((H_FOX, 1), F32),
                        pltpu.VMEM((H_FOX, W_FOX), F32), pltpu.VMEM((H_FOX, 1), F32)],
    )
    return pl.pallas_call(
        _fox_dec_kernel,
        grid_spec=grid_spec,
        out_shape=jax.ShapeDtypeStruct((bd, H_FOX, W_FOX), F32),
        compiler_params=_cparams(("arbitrary", "arbitrary")),
        name="fox_decode",
    )(page_table, qm, kv_new, lf_new_t, cache_kv, cache_lf_t, m_suf)


def _dsa_dec_kernel(pt_ref, iq_ref, iw_ref, ikn_ref, qm_ref, dkvn_ref, ik_ref, kv_ref, us_ref, ls_ref, o_ref,
                    sc_scr, sel_scr, self_scr, m_scr, l_scr, acc_scr, *, n_pages, k_sel):
    j = pl.program_id(1)
    page = ik_ref.shape[0]
    w = iw_ref[...] * (H_IDX ** -0.5)
    iq = iq_ref[...]

    @pl.when(j < n_pages)
    def _():
        r = jnp.maximum(_dot_nt(iq, ik_ref[...].astype(CD)) * (D_IDX ** -0.5), 0.0)
        sc_scr[pl.ds(j, 1), :] = jnp.sum(w * r, axis=0, keepdims=True)

    @pl.when(j == n_pages)
    def _():
        ikn = ikn_ref[...].astype(CD).astype(F32)
        r_self = jnp.maximum(jnp.sum(iq.astype(F32) * ikn, axis=1, keepdims=True) * (D_IDX ** -0.5), 0.0)
        s_self = jnp.sum(w * r_self, axis=0, keepdims=True)
        key = _sort_key(sc_scr[...])
        key_self = _sort_key(s_self)

        def count_ge(cand):
            c = jnp.sum(jnp.where(key >= cand, 1.0, 0.0), axis=1, keepdims=True)
            return jnp.sum(c, axis=0, keepdims=True) + jnp.where(key_self >= cand, 1.0, 0.0)

        thr = _kth_largest_key(count_ge, (1, 1), float(k_sel))
        gt = key > thr
        eq = key == thr
        eqf = jnp.where(eq, 1.0, 0.0)
        row_gt = jnp.sum(jnp.where(gt, 1.0, 0.0), axis=1, keepdims=True)
        n_gt = jnp.sum(row_gt, axis=0, keepdims=True) + jnp.where(key_self > thr, 1.0, 0.0)
        need = float(k_sel) - n_gt
        row_eq = jnp.sum(eqf, axis=1, keepdims=True)
        before = _dot(ls_ref[...], jnp.broadcast_to(row_eq, (n_pages, page)).astype(CD))
        rank = _dot(eqf.astype(CD), us_ref[...]) + before
        sel_scr[...] = jnp.where(gt | (eq & (rank < need)), 1.0, 0.0)
        n_eq_past = jnp.sum(row_eq, axis=0, keepdims=True)
        self_sel = (key_self > thr) | ((key_self == thr) & (n_eq_past < need))
        self_scr[...] = jnp.where(self_sel, 1.0, 0.0)
        qm = qm_ref[...].astype(F32)
        kn = dkvn_ref[:, 0:W_DKV].astype(CD).astype(F32)
        vn = dkvn_ref[:, W_DKV:2 * W_DKV].astype(CD).astype(F32)
        s_new = jnp.sum(qm * kn, axis=1, keepdims=True) * (D_HEAD ** -0.5)
        m_scr[...] = jnp.where(self_sel, s_new, NEG_INF)
        l_scr[...] = jnp.where(self_sel, jnp.ones_like(s_new), 0.0)
        acc_scr[...] = jnp.where(self_sel, jnp.broadcast_to(vn, acc_scr.shape), 0.0)

    @pl.when(j >= n_pages)
    def _():
        jj = j - n_pages
        kp = kv_ref[:, 0:W_DKV].astype(CD)
        vp = kv_ref[:, W_DKV:2 * W_DKV].astype(CD)
        hh = lax.broadcasted_iota(jnp.int32, (H_DSA, page), 0)
        slope = jnp.exp2(-8.0 * (hh + 1).astype(F32) / H_DSA)
        kpos = jj * page + lax.broadcasted_iota(jnp.int32, (H_DSA, page), 1)
        dist = (n_pages * page - kpos).astype(F32)
        s = _dot_nt(qm_ref[...], kp) * (D_HEAD ** -0.5) - slope * dist
        s = jnp.where(sel_scr[pl.ds(jj, 1), :] > 0.0, s, NEG_INF)
        m_prev = m_scr[...]
        m_new = jnp.maximum(m_prev, jnp.max(s, axis=1, keepdims=True))
        m_safe = jnp.where(m_new == NEG_INF, 0.0, m_new)
        alpha = jnp.exp(m_prev - m_safe)
        p = jnp.exp(s - m_safe)
        l_scr[...] = alpha * l_scr[...] + jnp.sum(p, axis=1, keepdims=True)
        acc_scr[...] = alpha * acc_scr[...] + _dot(p.astype(CD), vp)
        m_scr[...] = m_new

    @pl.when(j == 2 * n_pages - 1)
    def _():
        o_ref[...] = acc_scr[...] / l_scr[...]


def _dsa_decode(page_table, iq, iw_col, ik_new, qm, dkv_new, cache_idx, cache_kv, u_strict, l_strict, layer):
    bd, n_pages = page_table.shape
    page = cache_idx.shape[2]
    k_sel = min(TOPK_DSA, (n_pages * page + 1) // 4)
    per_b = lambda shape: pl.BlockSpec((None,) + shape, lambda b, j, pt: (b, 0, 0))
    grid_spec = pltpu.PrefetchScalarGridSpec(
        num_scalar_prefetch=1,
        grid=(bd, 2 * n_pages),
        in_specs=[
            per_b((H_IDX, D_IDX)), per_b((H_IDX, 1)), per_b((1, D_IDX)),
            per_b((H_DSA, W_DKV)), per_b((1, 2 * W_DKV)),
            pl.BlockSpec((None, None, page, D_IDX),
                         lambda b, j, pt: (layer, pt[b, jnp.minimum(j, n_pages - 1)], 0, 0)),
            pl.BlockSpec((None, None, page, 2 * W_DKV),
                         lambda b, j, pt: (layer, pt[b, jnp.maximum(j - n_pages, 0)], 0, 0)),
            pl.BlockSpec((page, page), lambda b, j, pt: (0, 0)),
            pl.BlockSpec((n_pages, n_pages), lambda b, j, pt: (0, 0)),
        ],
        out_specs=per_b((H_DSA, W_DKV)),
        scratch_shapes=[pltpu.VMEM((n_pages, page), F32), pltpu.VMEM((n_pages, page), F32),
                        pltpu.VMEM((1, 1), F32), pltpu.VMEM((H_DSA, 1), F32), pltpu.VMEM((H_DSA, 1), F32),
                        pltpu.VMEM((H_DSA, W_DKV), F32)],
    )
    return pl.pallas_call(
        functools.partial(_dsa_dec_kernel, n_pages=n_pages, k_sel=k_sel),
        grid_spec=grid_spec,
        out_shape=jax.ShapeDtypeStruct((bd, H_DSA, W_DKV), F32),
        compiler_params=_cparams(("arbitrary", "arbitrary")),
        name="dsa_decode",
    )(page_table, iq, iw_col, ik_new, qm, dkv_new, cache_idx, cache_kv, u_strict, l_strict)


def _perm_w_in(w):
    o = 0
    segs = {}
    for name, width in (("fq", W_FOX), ("fk", W_FOX), ("fv", W_FOX), ("ff", H_FOX), ("dq", W_DSA),
                        ("dk", W_DKV), ("dv", W_DKV), ("iq", W_IQ), ("ik", D_IDX), ("iw", H_IDX)):
        segs[name] = w[:, o:o + width]
        o += width
    pad = jnp.zeros((w.shape[0], LANES - D_IDX - H_FOX - H_IDX), w.dtype)
    return jnp.concatenate([segs["fq"], segs["fk"], segs["fv"], segs["dq"], segs["iq"], segs["dk"], segs["dv"],
                            segs["ik"], segs["ff"], segs["iw"], pad], axis=1)


def _heads(a, n):
    b, t, _ = a.shape
    return jnp.transpose(a.reshape(b, t, n, D_HEAD), (0, 2, 1, 3))


def _unheads(a):
    b, n, t, dh = a.shape
    return jnp.transpose(a, (0, 2, 1, 3)).reshape(b, t, n * dh)


def _block_diag_q(q, n_heads, n_groups):
    b = q.shape[0]
    qh = q.reshape(b, n_heads, 1, D_HEAD)
    grp = jnp.arange(n_heads) // (n_heads // n_groups)
    onehot = (grp[:, None] == jnp.arange(n_groups)[None, :]).astype(q.dtype)
    return (qh * onehot[None, :, :, None]).reshape(b, n_heads, n_groups * D_HEAD)


def _tile(n):
    for t in (512, 256, 128):
        if n % t == 0:
            return t
    return n


def kernel(x_prompt, x_sample, c_prompt, c_sample, cache_fox_kv, cache_fox_logf, cache_dsa_kv, cache_dsa_idx,
           page_table, w_ada, b_ada, g_attn, g_ffn, w_in, b_f, qn_fox, kn_fox, qn_dsa, kn_dsa, w_out,
           peer_wq, peer_k1, peer_k2, peer_u, peer_v):
    bsz, t, d = x_prompt.shape
    bd, ts, _ = x_sample.shape
    assert ts == 1 and t % LANES == 0
    depth = w_in.shape[0]
    n_pool, page = cache_fox_kv.shape[1], cache_fox_kv.shape[2]
    n_pages = page_table.shape[1]

    ii = np.arange(LANES)
    u_incl = jnp.asarray(ii[:, None] <= ii[None, :], CD)
    u_strict = jnp.asarray(ii[:, None] < ii[None, :], CD)
    pp = np.arange(page)
    m_suf = jnp.asarray(pp[:, None] > pp[None, :], CD)
    pg = np.arange(n_pages)
    l_strict = jnp.asarray(pg[None, :] < pg[:, None], CD)
    hh = np.arange(W_FOX) // D_HEAD
    seg = jnp.asarray(hh[:, None] == hh[None, :], CD)

    rows = bsz + bd
    rpad = -rows % 8
    c_all = jnp.concatenate([c_prompt, c_sample, jnp.zeros((rpad, d), F32)], axis=0)
    mods = _ada(c_all, w_ada, b_ada)

    cache_fkv = cache_fox_kv.reshape(depth, n_pool, page, 2 * W_FOX)
    cache_lft = jnp.transpose(cache_fox_logf, (0, 1, 3, 2))
    cache_dkv = cache_dsa_kv.reshape(depth, n_pool, page, 2 * W_DKV)

    xp = x_prompt
    xs = x_sample.reshape(1, bd, d)
    tm_p = _tile(t)
    outs = [[] for _ in range(8)]
    for l in range(depth):
        mp = [mods[l, :bsz, i * d:(i + 1) * d].reshape(bsz, 1, d) for i in range(6)]
        ms = [mods[l, bsz:rows, i * d:(i + 1) * d].reshape(1, bd, d) for i in range(6)]
        w_perm = _perm_w_in(w_in[l]).astype(CD)
        bias_row = jnp.zeros((1, LANES), F32).at[0, M_FF:M_FF + H_FOX].set(b_f[l])
        gains = jnp.stack([jnp.tile(qn_fox[l], H_FOX), jnp.tile(kn_fox[l], H_FOX),
                           jnp.tile(qn_dsa[l], H_DSA), jnp.tile(kn_dsa[l], H_DSA)])
        g_a = g_attn[l].reshape(1, d)
        g_f = g_ffn[l].reshape(1, d)
        w_o = w_out[l].astype(CD)

        fkv, dkv, misc, att = _proj(xp, mp[0], mp[1], g_a, w_perm, bias_row, gains, seg, min(tm_p, 256))
        logf = misc[..., M_FF:M_FF + H_FOX]
        csum = _cumsum(jnp.transpose(logf, (0, 2, 1)), u_incl)
        o_f = _fox_prompt(_heads(att[..., C_FQ:C_FQ + W_FOX], H_FOX), _heads(att[..., C_FK:C_FK + W_FOX], H_FOX),
                          _heads(att[..., C_FV:C_FV + W_FOX], H_FOX), csum[..., None], csum[:, :, None, :], tm_p)
        o_d = _dsa_prompt(_heads(att[..., C_IQ:C_IQ + W_IQ], H_IDX), misc[..., M_IW:M_IW + H_IDX],
                          att[..., C_MISC + M_IK:C_MISC + M_IK + D_IDX],
                          _heads(att[..., C_DQ:C_DQ + W_DSA], H_DSA), _heads(att[..., C_DK:C_DK + W_DKV], KV_DSA),
                          _heads(att[..., C_DV:C_DV + W_DKV], KV_DSA), u_strict, LANES)
        xp = _merge(_unheads(o_f), _unheads(o_d), w_o, mp[2], xp, tm_p)
        outs[0].append(fkv.reshape(bsz, t, 2, H_FOX, D_HEAD))
        outs[1].append(logf)
        outs[2].append(dkv.reshape(bsz, t, 2, KV_DSA, D_HEAD))
        outs[3].append(misc[..., M_IK:M_IK + D_IDX])

        fkv_s, dkv_s, misc_s, att_s = _proj(xs, ms[0], ms[1], g_a, w_perm, bias_row, gains, seg, bd)
        logf_s = misc_s[0, :, M_FF:M_FF + H_FOX]
        qm_f = _block_diag_q(att_s[0, :, C_FQ:C_FQ + W_FOX], H_FOX, H_FOX)
        o_fs = _fox_decode(page_table, qm_f, fkv_s.reshape(bd, 1, 2 * W_FOX), logf_s[:, :, None],
                           cache_fkv, cache_lft, m_suf, l)
        o_fs = jnp.einsum("bhhd->bhd", o_fs.reshape(bd, H_FOX, H_FOX, D_HEAD)).reshape(1, bd, W_FOX)
        qm_d = _block_diag_q(att_s[0, :, C_DQ:C_DQ + W_DSA], H_DSA, KV_DSA)
        o_ds = _dsa_decode(page_table, att_s[0, :, C_IQ:C_IQ + W_IQ].reshape(bd, H_IDX, D_IDX),
                           misc_s[0, :, M_IW:M_IW + H_IDX][:, :, None], misc_s[0, :, None, M_IK:M_IK + D_IDX],
                           qm_d, dkv_s.reshape(bd, 1, 2 * W_DKV), cache_dsa_idx, cache_dkv, u_strict, l_strict, l)
        grp = np.arange(H_DSA) // (H_DSA // KV_DSA)
        o_ds = o_ds.reshape(bd, H_DSA, KV_DSA, D_HEAD)[:, np.arange(H_DSA), grp].reshape(1, bd, W_DSA)
        xs = _merge(o_fs, o_ds, w_o, ms[2], xs, bd)
        outs[4].append(fkv_s.reshape(bd, 1, 2, H_FOX, D_HEAD))
        outs[5].append(logf_s.reshape(bd, 1, H_FOX))
        outs[6].append(dkv_s.reshape(bd, 1, 2, KV_DSA, D_HEAD))
        outs[7].append(misc_s[0, :, M_IK:M_IK + D_IDX].reshape(bd, 1, D_IDX))

        wq_t = jnp.transpose(peer_wq[l]).astype(CD)
        k1p = jnp.pad(peer_k1[l], ((0, 0), (0, 0), (0, D_KEY // 2)))
        k2p = jnp.pad(peer_k2[l], ((0, 0), (0, 0), (D_KEY // 2, 0)))
        kh = jnp.concatenate([k1p, k2p], axis=1)
        eye = jnp.eye(PEER_HEADS, dtype=kh.dtype)
        kbd_t = (kh[:, :, None, :] * eye[:, None, :, None]).reshape(PEER_HEADS * 2 * N_KEYS,
                                                                    PEER_HEADS * D_KEY).astype(CD)
        u_c = peer_u[l].astype(CD)
        v_c = peer_v[l].astype(CD)
        hq, st, et, tau = _peer_a(xp, mp[3], mp[4], g_f, wq_t, kbd_t, min(tm_p, 256))
        xp = _peer_b(hq, st, et, tau, u_c, v_c, xp, mp[5], tm_p, 2)
        hq, st, et, tau = _peer_a(xs, ms[3], ms[4], g_f, wq_t, kbd_t, bd)
        xs = _peer_b(hq, st, et, tau, u_c, v_c, xs, ms[5], bd, 2)

    stk = [jnp.stack(o) for o in outs]
    return (xp, xs.reshape(bd, 1, d), stk[0], stk[1], stk[2], stk[3], stk[4], stk[5], stk[6], stk[7])
```

```python
import functools

import jax
import jax.numpy as jnp
import numpy as np
from jax import lax
from jax.experimental import pallas as pl
from jax.experimental.pallas import tpu as pltpu

D_HEAD = 64
H_FOX = 8
H_DSA = 8
KV_DSA = 2
H_IDX = 8
D_IDX = 64
TOPK_DSA = 256
PEER_HEADS = 8
N_KEYS = 128
D_KEY = 128
PEER_TOPK = 16
EPS = 1e-6

LANES = 128
F32 = jnp.float32
CD = jnp.bfloat16
VMEM_LIMIT = 48 * 1024 * 1024

W_FOX = H_FOX * D_HEAD
W_DSA = H_DSA * D_HEAD
W_DKV = KV_DSA * D_HEAD
W_IQ = H_IDX * D_IDX
C_FQ, C_FK, C_FV, C_DQ, C_IQ = 0, 512, 1024, 1536, 2048
C_DK, C_DV, C_MISC, N_PERM = 2560, 2688, 2816, 2944
M_IK, M_FF, M_IW = 0, 64, 72

NEG_INF = float("-inf")
INT_MIN = -2147483648
KEY_NEG_INF = -2139095041
POS_RADIX = 64
QK_SCALE = D_HEAD ** -0.5

TILE_ROWS = 512
PROJ_ROWS = 256
DSA_TQ = 256
DSA_KC = 512
PEER_IC = 4
PEER_SUB = 32
DEC_PAGES = 8


def _cparams(sem):
    return pltpu.CompilerParams(dimension_semantics=sem, vmem_limit_bytes=VMEM_LIMIT)


def _dot(a, b):
    return jnp.dot(a, b, preferred_element_type=F32)


def _dot_nt(a, b):
    return lax.dot_general(a, b, (((1,), (1,)), ((), ())), preferred_element_type=F32)


def _dot_tn(a, b):
    return lax.dot_general(a, b, (((0,), (0,)), ((), ())), preferred_element_type=F32)


def _dot_split(x, m, n_split, dot=_dot):
    acc = None
    r = x
    for i in range(n_split):
        p = r.astype(CD)
        d = dot(p, m)
        acc = d if acc is None else acc + d
        if i + 1 < n_split:
            r = r - p.astype(F32)
    return acc


def _split3(x):
    p1 = x.astype(CD)
    r = x - p1.astype(F32)
    p2 = r.astype(CD)
    p3 = (r - p2.astype(F32)).astype(CD)
    return p1, p2, p3


def _rms_mod(x, g, shift, scale):
    ms = jnp.mean(x * x, axis=-1, keepdims=True)
    y = x * lax.rsqrt(ms + EPS) * g
    return y * (1.0 + scale) + shift


def _softmax_step(s, v, m_ref, l_ref, acc_ref):
    m_prev = m_ref[...]
    m_new = jnp.maximum(m_prev, jnp.max(s, axis=1, keepdims=True))
    m_safe = jnp.where(m_new == NEG_INF, 0.0, m_new)
    alpha = jnp.exp(m_prev - m_safe)
    p = jnp.exp(s - m_safe)
    l_ref[...] = alpha * l_ref[...] + jnp.sum(p, axis=1, keepdims=True)
    acc_ref[...] = alpha * acc_ref[...] + _dot(p.astype(CD), v)
    m_ref[...] = m_new


def _ada_kernel(c_ref, w_ref, b_ref, o_ref):
    c = c_ref[...]
    a = (c * jax.nn.sigmoid(c)).astype(CD)
    o_ref[...] = _dot(a, w_ref[...].astype(CD)) + b_ref[...]


def _ada(c_all, w_ada, b_ada):
    depth, d, n6 = w_ada.shape
    r = c_all.shape[0]
    tn = n6 // 4
    return pl.pallas_call(
        _ada_kernel,
        grid=(depth, n6 // tn),
        in_specs=[
            pl.BlockSpec((r, d), lambda l, j: (0, 0)),
            pl.BlockSpec((None, d, tn), lambda l, j: (l, 0, j)),
            pl.BlockSpec((None, 1, tn), lambda l, j: (l, 0, j)),
        ],
        out_specs=pl.BlockSpec((None, r, tn), lambda l, j: (l, 0, j)),
        out_shape=jax.ShapeDtypeStruct((depth, r, n6), F32),
        compiler_params=_cparams(("arbitrary", "arbitrary")),
        name="ada",
    )(c_all, w_ada, b_ada.reshape(depth, 1, n6))


def _head_norm(z, gain, seg):
    sq = z * z
    ss = _dot_split(sq, seg, 2)
    return z * lax.rsqrt(ss * (1.0 / D_HEAD) + EPS) * gain


def _proj_kernel(x_ref, sh_ref, sc_ref, g_ref, w_ref, bias_ref, gains_ref, seg_ref,
                 fkv_ref, dkv_ref, misc_ref, att_ref):
    h = _rms_mod(x_ref[...], g_ref[...], sh_ref[...], sc_ref[...]).astype(CD)
    seg = seg_ref[...]

    def seg_dot(c0, width):
        return _dot(h, w_ref[:, c0:c0 + width])

    fq = _head_norm(seg_dot(C_FQ, W_FOX), gains_ref[0:1, :], seg)
    att_ref[:, C_FQ:C_FQ + W_FOX] = fq.astype(CD)
    fk = _head_norm(seg_dot(C_FK, W_FOX), gains_ref[1:2, :], seg)
    fkv_ref[:, 0:W_FOX] = fk
    att_ref[:, C_FK:C_FK + W_FOX] = fk.astype(CD)
    fv = seg_dot(C_FV, W_FOX)
    fkv_ref[:, W_FOX:2 * W_FOX] = fv
    att_ref[:, C_FV:C_FV + W_FOX] = fv.astype(CD)
    dq = _head_norm(seg_dot(C_DQ, W_DSA), gains_ref[2:3, :], seg)
    att_ref[:, C_DQ:C_DQ + W_DSA] = dq.astype(CD)
    att_ref[:, C_IQ:C_IQ + W_IQ] = seg_dot(C_IQ, W_IQ).astype(CD)
    dk = _head_norm(seg_dot(C_DK, W_DKV), gains_ref[3:4, 0:W_DKV], seg[0:W_DKV, 0:W_DKV])
    dkv_ref[:, 0:W_DKV] = dk
    att_ref[:, C_DK:C_DK + W_DKV] = dk.astype(CD)
    dv = seg_dot(C_DV, W_DKV)
    dkv_ref[:, W_DKV:2 * W_DKV] = dv
    att_ref[:, C_DV:C_DV + W_DKV] = dv.astype(CD)
    zm = seg_dot(C_MISC, LANES)
    zb = zm + bias_ref[...]
    logsig = jnp.minimum(zb, 0.0) - jnp.log1p(jnp.exp(-jnp.abs(zb)))
    lane = lax.broadcasted_iota(jnp.int32, zm.shape, 1)
    misc = jnp.where((lane >= M_FF) & (lane < M_FF + H_FOX), logsig, zm)
    misc_ref[...] = misc
    att_ref[:, C_MISC:C_MISC + LANES] = misc.astype(CD)


def _mod_spec(mod, tm, d):
    if mod.shape[1] == 1:
        return pl.BlockSpec((None, 1, d), lambda b, i: (b, 0, 0))
    return pl.BlockSpec((None, tm, d), lambda b, i: (b, i, 0))


def _proj(x, shift, scale, g, w_perm, bias_row, gains, seg, tm):
    bsz, t, d = x.shape
    row = lambda width: pl.BlockSpec((None, tm, width), lambda b, i: (b, i, 0))
    const = lambda shape: pl.BlockSpec(shape, lambda b, i: (0, 0))
    return pl.pallas_call(
        _proj_kernel,
        grid=(bsz, t // tm),
        in_specs=[row(d), _mod_spec(shift, tm, d), _mod_spec(scale, tm, d), const((1, d)),
                  const((d, N_PERM)), const((1, LANES)), const((4, W_FOX)), const((W_FOX, W_FOX))],
        out_specs=[row(2 * W_FOX), row(2 * W_DKV), row(LANES), row(N_PERM)],
        out_shape=[jax.ShapeDtypeStruct((bsz, t, 2 * W_FOX), F32),
                   jax.ShapeDtypeStruct((bsz, t, 2 * W_DKV), F32),
                   jax.ShapeDtypeStruct((bsz, t, LANES), F32),
                   jax.ShapeDtypeStruct((bsz, t, N_PERM), CD)],
        compiler_params=_cparams(("arbitrary", "arbitrary")),
        name="proj",
    )(x, shift, scale, g, w_perm, bias_row, gains, seg)


def _cumsum_kernel(x_ref, u_ref, o_ref):
    t = x_ref.shape[-1]
    u = u_ref[...]
    carry = jnp.zeros((x_ref.shape[0], 1), F32)
    for j in range(t // LANES):
        c = _dot_split(x_ref[:, j * LANES:(j + 1) * LANES], u, 3) + carry
        o_ref[:, j * LANES:(j + 1) * LANES] = c
        carry = c[:, LANES - 1:LANES]


def _cumsum(lf_t, u_incl):
    bsz, h, t = lf_t.shape
    return pl.pallas_call(
        _cumsum_kernel,
        grid=(bsz,),
        in_specs=[pl.BlockSpec((None, h, t), lambda b: (b, 0, 0)),
                  pl.BlockSpec((LANES, LANES), lambda b: (0, 0))],
        out_specs=pl.BlockSpec((None, h, t), lambda b: (b, 0, 0)),
        out_shape=jax.ShapeDtypeStruct((bsz, h, t), F32),
        compiler_params=_cparams(("arbitrary",)),
        name="cumsum",
    )(lf_t, u_incl)


def _fox_kernel(q_ref, k_ref, v_ref, o_ref, m_scr, l_scr, acc_scr, *, tq, tk):
    qi = pl.program_id(2)
    ki = pl.program_id(3)

    @pl.when(ki == 0)
    def _():
        m_scr[...] = jnp.full(m_scr.shape, NEG_INF, F32)
        l_scr[...] = jnp.zeros(l_scr.shape, F32)
        acc_scr[...] = jnp.zeros(acc_scr.shape, F32)

    def step(on_diagonal):
        s = _dot_nt(q_ref[...], k_ref[...])
        if on_diagonal:
            qpos = lax.broadcasted_iota(jnp.int32, s.shape, 0)
            kpos = lax.broadcasted_iota(jnp.int32, s.shape, 1)
            s = jnp.where(kpos <= qpos, s, NEG_INF)
        _softmax_step(s, v_ref[...], m_scr, l_scr, acc_scr)

    @pl.when(ki < qi)
    def _():
        step(False)

    @pl.when(ki == qi)
    def _():
        step(True)

    @pl.when(ki == pl.num_programs(3) - 1)
    def _():
        o_ref[...] = acc_scr[...] / l_scr[...]


def _fox_prompt(q_aug, k_aug, v, tq):
    bsz, h, t, dk = q_aug.shape
    dh = v.shape[-1]
    nq = t // tq
    kmap = lambda b, hh, i, j: (b, hh, jnp.minimum(i, j), 0)
    return pl.pallas_call(
        functools.partial(_fox_kernel, tq=tq, tk=tq),
        grid=(bsz, h, nq, nq),
        in_specs=[pl.BlockSpec((None, None, tq, dk), lambda b, hh, i, j: (b, hh, i, 0)),
                  pl.BlockSpec((None, None, tq, dk), kmap),
                  pl.BlockSpec((None, None, tq, dh), kmap)],
        out_specs=pl.BlockSpec((None, None, tq, dh), lambda b, hh, i, j: (b, hh, i, 0)),
        out_shape=jax.ShapeDtypeStruct((bsz, h, t, dh), F32),
        scratch_shapes=[pltpu.VMEM((tq, 1), F32), pltpu.VMEM((tq, 1), F32), pltpu.VMEM((tq, dh), F32)],
        compiler_params=_cparams(("arbitrary",) * 4),
        name="fox_prompt",
    )(q_aug, k_aug, v)


def _sort_key(x):
    b = pltpu.bitcast(x, jnp.int32)
    return b ^ (lax.shift_right_arithmetic(b, 31) & jnp.int32(0x7FFFFFFF))


def _kth_largest_key(count_ge, shape, k):
    def body(i, res):
        cand = res + lax.shift_left(jnp.int32(1), jnp.int32(31) - i)
        return jnp.where(count_ge(cand) >= k, cand, res)

    return lax.fori_loop(0, 32, body, jnp.full(shape, INT_MIN, jnp.int32))


def _count(mask):
    return jnp.sum(jnp.where(mask, 1.0, 0.0), axis=1, keepdims=True)


def _lane_fold(x, op):
    return functools.reduce(op, [x[:, j * LANES:(j + 1) * LANES] for j in range(x.shape[1] // LANES)])


def _dsa_kernel(iq_ref, iw_ref, ik_ref, dq_ref, dk_ref, dv_ref, us_ref, o_ref,
                key_scr, selb_scr, m_scr, l_scr, acc_scr, *, tq, kc, k_sel):
    qi = pl.program_id(1)
    n_c = (qi * tq + tq + kc - 1) // kc
    w = iw_ref[...] * (H_IDX ** -0.5) * (D_IDX ** -0.5)
    qpos = qi * tq + lax.broadcasted_iota(jnp.int32, (tq, kc), 0)
    kk = float(k_sel)

    def score_body(c, carry):
        k0 = pl.multiple_of(c * kc, kc)
        ikc = ik_ref[pl.ds(k0, kc), :]
        score = jnp.zeros((tq, kc), F32)
        for h in range(H_IDX):
            score = score + w[:, h:h + 1] * jnp.maximum(_dot_nt(iq_ref[h], ikc), 0.0)
        kpos = k0 + lax.broadcasted_iota(jnp.int32, (tq, kc), 1)
        key_scr[c] = _sort_key(jnp.where(kpos <= qpos, score, NEG_INF))
        return carry

    lax.fori_loop(0, n_c, score_body, 0)

    def chunk_count(pred):
        part = lax.fori_loop(
            0, n_c, lambda c, a: a + _lane_fold(jnp.where(pred(key_scr[c]), 1.0, 0.0), jnp.add),
            jnp.zeros((tq, LANES), F32))
        return jnp.sum(part, axis=1, keepdims=True)

    thr = _kth_largest_key(lambda cand: chunk_count(lambda key: key >= cand), (tq, 1), kk)
    is_gt = lambda key: (key > thr) & (key != KEY_NEG_INF)
    is_eq = lambda key: (key == thr) & (key != KEY_NEG_INF)
    need = kk - chunk_count(is_gt)
    n_eq = chunk_count(is_eq)

    def selb_body(c, carry):
        key = key_scr[c]
        selb_scr[c] = jnp.where(is_gt(key) | is_eq(key), 0.0, NEG_INF)
        return carry

    lax.fori_loop(0, n_c, selb_body, 0)

    @pl.when(jnp.max(n_eq - need) > 0.0)
    def _():
        us = us_ref[...]

        def tie_body(c, before):
            key = key_scr[c]
            eq = is_eq(key)
            eqf = jnp.where(eq, 1.0, 0.0).astype(CD)
            ranks = []
            for j in range(kc // LANES):
                blk = eqf[:, j * LANES:(j + 1) * LANES]
                ranks.append(_dot(blk, us) + before)
                before = before + jnp.sum(blk.astype(F32), axis=1, keepdims=True)
            rank = jnp.concatenate(ranks, axis=1)
            selb_scr[c] = jnp.where(is_gt(key) | (eq & (rank < need)), 0.0, NEG_INF)
            return before

        lax.fori_loop(0, n_c, tie_body, jnp.zeros((tq, 1), F32))

    m_scr[...] = jnp.full(m_scr.shape, NEG_INF, F32)
    l_scr[...] = jnp.zeros(l_scr.shape, F32)
    acc_scr[...] = jnp.zeros(acc_scr.shape, F32)
    rep = H_DSA // KV_DSA

    def logits(c, h):
        k0 = pl.multiple_of(c * kc, kc)
        return _dot_nt(dq_ref[h], dk_ref[h // rep, pl.ds(k0, kc), :]) + selb_scr[c]

    def max_body(c, carry):
        for h in range(H_DSA):
            m_scr[h] = jnp.maximum(m_scr[h], _lane_fold(logits(c, h), jnp.maximum))
        return carry

    lax.fori_loop(0, n_c, max_body, 0)
    for h in range(H_DSA):
        m_scr[h] = jnp.broadcast_to(jnp.max(m_scr[h], axis=1, keepdims=True), (tq, LANES))

    def att_body(c, carry):
        k0 = pl.multiple_of(c * kc, kc)
        for h in range(H_DSA):
            s = logits(c, h)
            m = m_scr[h]
            p = [jnp.exp(s[:, j * LANES:(j + 1) * LANES] - m) for j in range(kc // LANES)]
            l_scr[h] = l_scr[h] + functools.reduce(jnp.add, p)
            acc_scr[h] = acc_scr[h] + _dot(jnp.concatenate(p, axis=1).astype(CD),
                                           dv_ref[h // rep, pl.ds(k0, kc), :])
        return carry

    lax.fori_loop(0, n_c, att_body, 0)
    for h in range(H_DSA):
        o_ref[h] = acc_scr[h] / jnp.sum(l_scr[h], axis=1, keepdims=True)


def _dsa_prompt(iq, iw, ik, dq_aug, dk_aug, dv, u_strict, tq, kc):
    bsz, hi, t, di = iq.shape
    dka = dq_aug.shape[-1]
    k_sel = min(TOPK_DSA, t // 4)
    qh = lambda nh, w: pl.BlockSpec((None, nh, tq, w), lambda b, i: (b, 0, i, 0))
    full = lambda nh, w: pl.BlockSpec((None, nh, t, w), lambda b, i: (b, 0, 0, 0))
    return pl.pallas_call(
        functools.partial(_dsa_kernel, tq=tq, kc=kc, k_sel=k_sel),
        grid=(bsz, t // tq),
        in_specs=[qh(hi, di),
                  pl.BlockSpec((None, tq, H_IDX), lambda b, i: (b, i, 0)),
                  pl.BlockSpec((None, t, di), lambda b, i: (b, 0, 0)),
                  qh(H_DSA, dka), full(KV_DSA, dka), full(KV_DSA, D_HEAD),
                  pl.BlockSpec((LANES, LANES), lambda b, i: (0, 0))],
        out_specs=qh(H_DSA, D_HEAD),
        out_shape=jax.ShapeDtypeStruct((bsz, H_DSA, t, D_HEAD), F32),
        scratch_shapes=[pltpu.VMEM((t // kc, tq, kc), jnp.int32), pltpu.VMEM((t // kc, tq, kc), F32),
                        pltpu.VMEM((H_DSA, tq, LANES), F32), pltpu.VMEM((H_DSA, tq, LANES), F32),
                        pltpu.VMEM((H_DSA, tq, D_HEAD), F32)],
        compiler_params=_cparams(("arbitrary", "arbitrary")),
        name="dsa_prompt",
    )(iq, iw, ik, dq_aug, dk_aug, dv, u_strict)


def _merge_kernel(of_ref, od_ref, w_ref, gate_ref, x_ref, o_ref):
    half = of_ref.shape[-1]
    y = _dot(of_ref[...].astype(CD), w_ref[0:half, :]) + _dot(od_ref[...].astype(CD), w_ref[half:, :])
    o_ref[...] = x_ref[...] + gate_ref[...] * y


def _merge(o_f, o_d, w_out, gate, x, tm):
    bsz, t, d = x.shape
    half = o_f.shape[-1]
    row = lambda width: pl.BlockSpec((None, tm, width), lambda b, i: (b, i, 0))
    return pl.pallas_call(
        _merge_kernel,
        grid=(bsz, t // tm),
        in_specs=[row(half), row(half), pl.BlockSpec((2 * half, d), lambda b, i: (0, 0)),
                  _mod_spec(gate, tm, d), row(d)],
        out_specs=row(d),
        out_shape=jax.ShapeDtypeStruct((bsz, t, d), F32),
        compiler_params=_cparams(("arbitrary", "arbitrary")),
        name="merge",
    )(o_f, o_d, w_out, gate, x)


def _top_rows(x, n):
    vals = []
    for _ in range(n):
        m = jnp.max(x, axis=0, keepdims=True)
        vals.append(m)
        x = jnp.where(x == m, NEG_INF, x)
    return vals


_PEER_PAIRS = [(i, j) for i in range(PEER_TOPK) for j in range(PEER_TOPK) if (i + 1) * (j + 1) <= PEER_TOPK]
_PEER_CAND_ROWS = -(-len(_PEER_PAIRS) // 8) * 8


def _peer_a_kernel(x_ref, sh_ref, sc_ref, g_ref, wqt_ref, kbdt_ref, hq_ref, cut_ref, e1_ref, s2_ref, e2_ref,
                   cand_scr, candw_scr):
    hq = _rms_mod(x_ref[...], g_ref[...], sh_ref[...], sc_ref[...]).astype(CD)
    hq_ref[...] = hq
    qt = _dot_nt(wqt_ref[...], hq)
    st = _dot(kbdt_ref[...], qt.astype(CD))
    cand_scr[...] = jnp.full(cand_scr.shape, NEG_INF, F32)
    candw_scr[...] = jnp.zeros(candw_scr.shape, F32)
    for h in range(PEER_HEADS):
        r0 = h * 2 * N_KEYS
        s1 = st[r0:r0 + N_KEYS]
        s2 = st[r0 + N_KEYS:r0 + 2 * N_KEYS]
        v1 = _top_rows(s1, PEER_TOPK)
        v2 = _top_rows(s2, PEER_TOPK)
        w1 = [jnp.exp(a - v1[0]) for a in v1]
        w2 = [jnp.exp(a - v2[0]) for a in v2]
        sums = {}
        for r, (i, j) in enumerate(_PEER_PAIRS):
            sums[i, j] = v1[i] + v2[j]
            cand_scr[r:r + 1, :] = sums[i, j]
            candw_scr[r:r + 1, :] = w1[i] * w2[j]
        cand = cand_scr[...]
        tau = _top_rows(cand, PEER_TOPK)[-1]
        z = jnp.sum(jnp.where(cand >= tau, candw_scr[...], 0.0), axis=0, keepdims=True)
        cut = jnp.full(s1.shape, jnp.inf, F32)
        for i in range(PEER_TOPK):
            cut_i = jnp.full(tau.shape, jnp.inf, F32)
            for j in range(PEER_TOPK // (i + 1)):
                cut_i = jnp.where(sums[i, j] >= tau, v2[j], cut_i)
            cut = jnp.where(s1 == v1[i], cut_i, cut)
        h0 = h * N_KEYS
        lbw = cut_ref.shape[-1]
        e1 = jnp.exp(s1 - v1[0])
        e2 = jnp.exp(s2 - v2[0]) / z
        for j in range(cut_ref.shape[0]):
            cols = slice(j * lbw, (j + 1) * lbw)
            cut_ref[j, h0:h0 + N_KEYS, :] = cut[:, cols]
            e1_ref[j, h0:h0 + N_KEYS, :] = e1[:, cols]
            s2_ref[j, h0:h0 + N_KEYS, :] = s2[:, cols]
            e2_ref[j, h0:h0 + N_KEYS, :] = e2[:, cols]


def _peer_a(x, shift, scale, g, wq_t, kbd_t, tm):
    bsz, t, d = x.shape
    ns = PEER_HEADS * N_KEYS
    lbw = min(LANES, tm)
    row = pl.BlockSpec((None, tm, d), lambda b, i: (b, i, 0))
    col = pl.BlockSpec((None, tm // lbw, ns, lbw), lambda b, i: (b, i, 0, 0))
    const = lambda shape: pl.BlockSpec(shape, lambda b, i: (0, 0))
    cols = jax.ShapeDtypeStruct((bsz, t // lbw, ns, lbw), F32)
    return pl.pallas_call(
        _peer_a_kernel,
        grid=(bsz, t // tm),
        in_specs=[row, _mod_spec(shift, tm, d), _mod_spec(scale, tm, d), const((1, d)),
                  const(wq_t.shape), const(kbd_t.shape)],
        out_specs=[row, col, col, col, col],
        out_shape=[jax.ShapeDtypeStruct((bsz, t, d), CD), cols, cols, cols, cols],
        scratch_shapes=[pltpu.VMEM((_PEER_CAND_ROWS, tm), F32), pltpu.VMEM((_PEER_CAND_ROWS, tm), F32)],
        compiler_params=_cparams(("arbitrary", "arbitrary")),
        name="peer_a",
    )(x, shift, scale, g, wq_t, kbd_t)


def _peer_b_kernel(hq_ref, cut_ref, e1_ref, s2_ref, e2_ref, u_ref, v_ref, x_ref, gate_ref, o_ref,
                   acc_scr, coef_scr, gel_scr, *, ic, sub):
    c = pl.program_id(2)
    nlb, _, lbw = cut_ref.shape

    @pl.when(c == 0)
    def _():
        acc_scr[...] = jnp.zeros(acc_scr.shape, F32)

    act = _dot_nt(u_ref[...], hq_ref[...])
    for j in range(nlb):
        a = act[:, j * lbw:(j + 1) * lbw]
        gel_scr[j] = 0.5 * a * (1.0 + lax.erf(a * (2.0 ** -0.5)))

    n_r = N_KEYS // sub

    def tile_body(t, carry):
        j = t // n_r
        r = pl.multiple_of((t % n_r) * sub, sub)
        gates = [jnp.zeros((sub, lbw), F32) for _ in range(ic)]
        for h in range(PEER_HEADS):
            s2 = s2_ref[j, pl.ds(h * N_KEYS + r, sub), :]
            e2 = e2_ref[j, pl.ds(h * N_KEYS + r, sub), :]
            for ii in range(ic):
                cut = cut_ref[j, pl.ds(h * N_KEYS + c * ic + ii, 1), :]
                e1 = e1_ref[j, pl.ds(h * N_KEYS + c * ic + ii, 1), :]
                gates[ii] = gates[ii] + jnp.where(s2 >= cut, e1 * e2, 0.0)
        for ii in range(ic):
            rows = pl.ds(ii * N_KEYS + r, sub)
            coef_scr[j, rows, :] = (gates[ii] * gel_scr[j, rows, :]).astype(CD)
        return carry

    lax.fori_loop(0, nlb * n_r, tile_body, 0)
    coef = jnp.concatenate([coef_scr[j] for j in range(nlb)], axis=1)
    acc_scr[...] += _dot_tn(coef, v_ref[...])

    @pl.when(c == pl.num_programs(2) - 1)
    def _():
        o_ref[...] = x_ref[...] + gate_ref[...] * acc_scr[...]


def _peer_b(hq, cut, e1, s2, e2, u_c, v_c, x, gate, tm, ic):
    bsz, t, d = x.shape
    _, _, ns, lbw = cut.shape
    nlb = tm // lbw
    ec = ic * N_KEYS
    row = pl.BlockSpec((None, tm, d), lambda b, i, c: (b, i, 0))
    col = pl.BlockSpec((None, nlb, ns, lbw), lambda b, i, c: (b, i, 0, 0))
    wspec = pl.BlockSpec((ec, d), lambda b, i, c: (c, 0))
    if gate.shape[1] == 1:
        gspec = pl.BlockSpec((None, 1, d), lambda b, i, c: (b, 0, 0))
    else:
        gspec = row
    return pl.pallas_call(
        functools.partial(_peer_b_kernel, ic=ic, sub=PEER_SUB),
        grid=(bsz, t // tm, N_KEYS // ic),
        in_specs=[row, col, col, col, col, wspec, wspec, row, gspec],
        out_specs=row,
        out_shape=jax.ShapeDtypeStruct((bsz, t, d), F32),
        scratch_shapes=[pltpu.VMEM((tm, d), F32), pltpu.VMEM((nlb, ec, lbw), CD),
                        pltpu.VMEM((nlb, ec, lbw), F32)],
        compiler_params=_cparams(("arbitrary",) * 3),
        name="peer_b",
    )(hq, cut, e1, s2, e2, u_c, v_c, x, gate)


def _fox_dec_kernel(pt_ref, q_ref, kvn_ref, lfn_ref, mexp_ref, ones_ref, *rest, pp, slots):
    kv_refs = rest[:pp]
    lf_refs = rest[pp:2 * pp]
    o_ref, m_scr, l_scr, acc_scr, carry_scr, kvc_scr = rest[2 * pp:]
    j = pl.program_id(1)
    q = q_ref[...]

    @pl.when(j == 0)
    def _():
        kn = kvn_ref[0:H_FOX, :].astype(CD).astype(F32)
        m_scr[...] = jnp.sum(q.astype(F32) * kn, axis=1, keepdims=True)
        l_scr[...] = jnp.ones(l_scr.shape, F32)
        acc_scr[...] = kvn_ref[H_FOX:2 * H_FOX, :].astype(CD).astype(F32)
        carry_scr[...] = lfn_ref[...]

    ncol = kv_refs[0].shape[0]
    row = lax.broadcasted_iota(jnp.int32, (H_FOX, ncol), 0)
    col = lax.broadcasted_iota(jnp.int32, (H_FOX, ncol), 1)
    own = (col & (slots - 1)) == row
    carry = carry_scr[...]
    logits = []
    for p in range(pp):
        kvc_scr[p] = kv_refs[p][...].astype(CD)
        lf = lf_refs[p][...]
        decay = _dot_split(lf, mexp_ref[...], 3, dot=_dot_tn) + carry
        logits.append(jnp.where(own, _dot_nt(q, kvc_scr[p]) + decay, NEG_INF))
        carry = carry + _dot_split(lf, ones_ref[...], 3, dot=_dot_tn)[:, 0:1]
    carry_scr[...] = carry
    top = logits[0]
    for s in logits[1:]:
        top = jnp.maximum(top, s)
    m_prev = m_scr[...]
    m_new = jnp.maximum(m_prev, jnp.max(top, axis=1, keepdims=True))
    alpha = jnp.exp(m_prev - m_new)
    acc = alpha * acc_scr[...]
    psum = jnp.zeros(top.shape, F32)
    for p in range(pp):
        prob = jnp.exp(logits[p] - m_new)
        psum = psum + prob
        acc = acc + _dot(pltpu.roll(prob, slots // 2, 1).astype(CD), kvc_scr[p])
    l_scr[...] = alpha * l_scr[...] + jnp.sum(psum, axis=1, keepdims=True)
    acc_scr[...] = acc
    m_scr[...] = m_new

    @pl.when(j == pl.num_programs(1) - 1)
    def _():
        o_ref[...] = acc_scr[...] / l_scr[...]


def _fox_decode(page_table, q, kv_new, lf_new, cache_rows, cache_lf, m_exp, ones, layer, pp):
    bd, n_pages = page_table.shape
    rows = cache_rows.shape[2]
    page = cache_lf.shape[2]
    slots = rows // page
    per_b = lambda shape: pl.BlockSpec((None,) + shape, lambda b, j, pt: (b, 0, 0))
    const = lambda shape: pl.BlockSpec(shape, lambda b, j, pt: (0, 0))

    def page_spec(shape, p):
        return pl.BlockSpec((None, None) + shape,
                            lambda b, j, pt: (layer, pt[b, n_pages - 1 - (j * pp + p)], 0, 0))

    grid_spec = pltpu.PrefetchScalarGridSpec(
        num_scalar_prefetch=1,
        grid=(bd, n_pages // pp),
        in_specs=[per_b((H_FOX, D_HEAD)), per_b((2 * H_FOX, D_HEAD)), per_b((H_FOX, 1)),
                  const(m_exp.shape), const(ones.shape)]
                 + [page_spec((rows, D_HEAD), p) for p in range(pp)]
                 + [page_spec((page, H_FOX), p) for p in range(pp)],
        out_specs=per_b((H_FOX, D_HEAD)),
        scratch_shapes=[pltpu.VMEM((H_FOX, 1), F32), pltpu.VMEM((H_FOX, 1), F32),
                        pltpu.VMEM((H_FOX, D_HEAD), F32), pltpu.VMEM((H_FOX, 1), F32),
                        pltpu.VMEM((pp, rows, D_HEAD), CD)],
    )
    return pl.pallas_call(
        functools.partial(_fox_dec_kernel, pp=pp, slots=slots),
        grid_spec=grid_spec,
        out_shape=jax.ShapeDtypeStruct((bd, H_FOX, D_HEAD), F32),
        compiler_params=_cparams(("arbitrary", "arbitrary")),
        name="fox_decode",
    )(page_table, q, kv_new, lf_new, m_exp, ones, *([cache_rows] * pp), *([cache_lf] * pp))


def _dsa_sel_kernel(pt_ref, iq_ref, iw_ref, ikn_ref, us_ref, ls_ref, *rest, pp, n_pages, k_sel):
    ik_refs = rest[:pp]
    selb_ref, self_ref, sc_scr = rest[pp:]
    j = pl.program_id(1)
    page = ik_refs[0].shape[0]
    w = iw_ref[...] * (H_IDX ** -0.5) * (D_IDX ** -0.5)
    iq = iq_ref[...]
    for p in range(pp):
        r = jnp.maximum(_dot_nt(iq, ik_refs[p][...].astype(CD)), 0.0)
        sc_scr[pl.ds(j * pp + p, 1), :] = jnp.sum(w * r, axis=0, keepdims=True)

    @pl.when(j == pl.num_programs(1) - 1)
    def _():
        ikn = ikn_ref[...].astype(CD).astype(F32)
        r_self = jnp.maximum(jnp.sum(iq.astype(F32) * ikn, axis=1, keepdims=True), 0.0)
        s_self = jnp.sum(w * r_self, axis=0, keepdims=True)
        key = _sort_key(sc_scr[...])
        key_self = _sort_key(s_self)
        total = lambda m: jnp.sum(_count(m), axis=0, keepdims=True)

        def count_ge(cand):
            return total(key >= cand) + jnp.where(key_self >= cand, 1.0, 0.0)

        thr = _kth_largest_key(count_ge, (1, 1), float(k_sel))
        gt = key > thr
        eq = key == thr
        eqf = jnp.where(eq, 1.0, 0.0)
        need = float(k_sel) - (total(gt) + jnp.where(key_self > thr, 1.0, 0.0))
        row_eq = jnp.sum(eqf, axis=1, keepdims=True)
        before = _dot(ls_ref[...], jnp.broadcast_to(row_eq, (n_pages, page)).astype(CD))
        rank = _dot(eqf.astype(CD), us_ref[...]) + before
        selb_ref[...] = jnp.where(gt | (eq & (rank < need)), 0.0, NEG_INF)
        n_eq_past = jnp.sum(row_eq, axis=0, keepdims=True)
        self_sel = (key_self > thr) | ((key_self == thr) & (n_eq_past < need))
        self_ref[...] = jnp.broadcast_to(jnp.where(self_sel, 1.0, 0.0), self_ref.shape)


def _dsa_select(page_table, iq, iw_col, ik_new, cache_idx, u_strict, l_strict, layer, pp):
    bd, n_pages = page_table.shape
    page = cache_idx.shape[2]
    k_sel = min(TOPK_DSA, (n_pages * page + 1) // 4)
    per_b = lambda shape: pl.BlockSpec((None,) + shape, lambda b, j, pt: (b, 0, 0))
    const = lambda shape: pl.BlockSpec(shape, lambda b, j, pt: (0, 0))
    grid_spec = pltpu.PrefetchScalarGridSpec(
        num_scalar_prefetch=1,
        grid=(bd, n_pages // pp),
        in_specs=[per_b((H_IDX, D_IDX)), per_b((H_IDX, 1)), per_b((1, D_IDX)),
                  const((page, page)), const((n_pages, n_pages))]
                 + [pl.BlockSpec((None, None, page, D_IDX),
                                 lambda b, j, pt, p=p: (layer, pt[b, j * pp + p], 0, 0)) for p in range(pp)],
        out_specs=[per_b((n_pages, page)), per_b((H_DSA, LANES))],
        scratch_shapes=[pltpu.VMEM((n_pages, page), F32)],
    )
    return pl.pallas_call(
        functools.partial(_dsa_sel_kernel, pp=pp, n_pages=n_pages, k_sel=k_sel),
        grid_spec=grid_spec,
        out_shape=[jax.ShapeDtypeStruct((bd, n_pages, page), F32),
                   jax.ShapeDtypeStruct((bd, H_DSA, LANES), F32)],
        compiler_params=_cparams(("arbitrary", "arbitrary")),
        name="dsa_select",
    )(page_table, iq, iw_col, ik_new, u_strict, l_strict, *([cache_idx] * pp))


def _dsa_dec_kernel(pt_ref, qm_ref, dkvn_ref, selb_ref, self_ref, *rest, pp, n_pages):
    kv_refs = rest[:pp]
    o_ref, m_scr, l_scr, acc_scr = rest[pp:]
    j = pl.program_id(1)
    page = kv_refs[0].shape[0]
    qm = qm_ref[...]

    @pl.when(j == 0)
    def _():
        self_sel = self_ref[:, 0:1] > 0.0
        kn = dkvn_ref[:, 0:W_DKV].astype(CD).astype(F32)
        vn = dkvn_ref[:, W_DKV:2 * W_DKV].astype(CD).astype(F32)
        s_new = jnp.sum(qm.astype(F32) * kn, axis=1, keepdims=True)
        m_scr[...] = jnp.where(self_sel, s_new, NEG_INF)
        l_scr[...] = jnp.where(self_sel, 1.0, 0.0)
        acc_scr[...] = jnp.where(self_sel, jnp.broadcast_to(vn, acc_scr.shape), 0.0)

    hh = lax.broadcasted_iota(jnp.int32, (H_DSA, page), 0)
    slope = jnp.exp2(-8.0 * (hh + 1).astype(F32) / H_DSA)
    lane = lax.broadcasted_iota(jnp.int32, (H_DSA, page), 1)
    logits = []
    for p in range(pp):
        pg = j * pp + p
        kp = kv_refs[p][:, 0:W_DKV].astype(CD)
        dist = (n_pages * page - (pg * page + lane)).astype(F32)
        logits.append(_dot_nt(qm, kp) - slope * dist + selb_ref[pl.ds(pg, 1), :])
    top = logits[0]
    for s in logits[1:]:
        top = jnp.maximum(top, s)
    m_prev = m_scr[...]
    m_new = jnp.maximum(m_prev, jnp.max(top, axis=1, keepdims=True))
    m_safe = jnp.where(m_new == NEG_INF, 0.0, m_new)
    alpha = jnp.exp(m_prev - m_safe)
    acc = alpha * acc_scr[...]
    psum = jnp.zeros(top.shape, F32)
    for p in range(pp):
        prob = jnp.exp(logits[p] - m_safe)
        psum = psum + prob
        acc = acc + _dot(prob.astype(CD), kv_refs[p][:, W_DKV:2 * W_DKV].astype(CD))
    l_scr[...] = alpha * l_scr[...] + jnp.sum(psum, axis=1, keepdims=True)
    acc_scr[...] = acc
    m_scr[...] = m_new

    @pl.when(j == pl.num_programs(1) - 1)
    def _():
        o_ref[...] = acc_scr[...] / l_scr[...]


def _dsa_decode(page_table, qm, dkv_new, selb, self_sel, cache_kv, layer, pp):
    bd, n_pages = page_table.shape
    page = cache_kv.shape[2]
    per_b = lambda shape: pl.BlockSpec((None,) + shape, lambda b, j, pt: (b, 0, 0))
    grid_spec = pltpu.PrefetchScalarGridSpec(
        num_scalar_prefetch=1,
        grid=(bd, n_pages // pp),
        in_specs=[per_b((H_DSA, W_DKV)), per_b((1, 2 * W_DKV)), per_b((n_pages, page)), per_b((H_DSA, LANES))]
                 + [pl.BlockSpec((None, None, page, 2 * W_DKV),
                                 lambda b, j, pt, p=p: (layer, pt[b, j * pp + p], 0, 0)) for p in range(pp)],
        out_specs=per_b((H_DSA, W_DKV)),
        scratch_shapes=[pltpu.VMEM((H_DSA, 1), F32), pltpu.VMEM((H_DSA, 1), F32),
                        pltpu.VMEM((H_DSA, W_DKV), F32)],
    )
    return pl.pallas_call(
        functools.partial(_dsa_dec_kernel, pp=pp, n_pages=n_pages),
        grid_spec=grid_spec,
        out_shape=jax.ShapeDtypeStruct((bd, H_DSA, W_DKV), F32),
        compiler_params=_cparams(("arbitrary", "arbitrary")),
        name="dsa_decode",
    )(page_table, qm, dkv_new, selb, self_sel, *([cache_kv] * pp))


def _perm_w_in(w):
    o = 0
    segs = {}
    for name, width in (("fq", W_FOX), ("fk", W_FOX), ("fv", W_FOX), ("ff", H_FOX), ("dq", W_DSA),
                        ("dk", W_DKV), ("dv", W_DKV), ("iq", W_IQ), ("ik", D_IDX), ("iw", H_IDX)):
        segs[name] = w[:, o:o + width]
        o += width
    pad = jnp.zeros((w.shape[0], LANES - D_IDX - H_FOX - H_IDX), w.dtype)
    return jnp.concatenate([segs["fq"], segs["fk"], segs["fv"], segs["dq"], segs["iq"], segs["dk"], segs["dv"],
                            segs["ik"], segs["ff"], segs["iw"], pad], axis=1)


def _heads(a, n):
    b, t, _ = a.shape
    return jnp.transpose(a.reshape(b, t, n, D_HEAD), (0, 2, 1, 3))


def _unheads(a):
    b, n, t, dh = a.shape
    return jnp.transpose(a, (0, 2, 1, 3)).reshape(b, t, n * dh)


def _augment(base, cols):
    shape = base.shape[:-1]
    cols = [jnp.broadcast_to(jnp.asarray(c, CD), shape + (1,)) for c in cols]
    pad = jnp.zeros(shape + (2 * D_HEAD - base.shape[-1] - len(cols),), CD)
    return jnp.concatenate([base.astype(CD)] + cols + [pad], axis=-1)


def _block_diag_q(q, n_heads, n_groups):
    b = q.shape[0]
    qh = q.reshape(b, n_heads, 1, D_HEAD)
    grp = jnp.arange(n_heads) // (n_heads // n_groups)
    onehot = (grp[:, None] == jnp.arange(n_groups)[None, :]).astype(q.dtype)
    return (qh * onehot[None, :, :, None]).reshape(b, n_heads, n_groups * D_HEAD)


def _tile(n, pref):
    for t in (pref, pref // 2, pref // 4):
        if t >= 8 and n % t == 0:
            return t
    return n


def kernel(x_prompt, x_sample, c_prompt, c_sample, cache_fox_kv, cache_fox_logf, cache_dsa_kv, cache_dsa_idx,
           page_table, w_ada, b_ada, g_attn, g_ffn, w_in, b_f, qn_fox, kn_fox, qn_dsa, kn_dsa, w_out,
           peer_wq, peer_k1, peer_k2, peer_u, peer_v):
    bsz, t, d = x_prompt.shape
    bd, ts, _ = x_sample.shape
    depth = w_in.shape[0]
    n_pool, page = cache_fox_kv.shape[1], cache_fox_kv.shape[2]
    n_pages = page_table.shape[1]
    assert ts == 1 and t % LANES == 0 and t < POS_RADIX * 256 and page == LANES
    tm_p = _tile(t, TILE_ROWS)
    tm_w = _tile(t, PROJ_ROWS)
    kc = _tile(t, DSA_KC)
    pp = _tile(n_pages, DEC_PAGES) if n_pages >= 8 else n_pages

    ii = np.arange(LANES)
    u_incl = jnp.asarray(ii[:, None] <= ii[None, :], CD)
    u_strict = jnp.asarray(ii[:, None] < ii[None, :], CD)
    pg = np.arange(n_pages)
    l_strict = jnp.asarray(pg[None, :] < pg[:, None], CD)
    hh = np.arange(W_FOX) // D_HEAD
    seg = jnp.asarray(hh[:, None] == hh[None, :], CD)
    fox_slots = 2 * H_FOX
    pos_of_col = np.arange(page * fox_slots) // fox_slots
    m_exp = jnp.asarray(np.arange(page)[:, None] > pos_of_col[None, :], CD)
    ones = jnp.ones((page, LANES), CD)

    rows = bsz + bd
    rpad = -rows % 8
    c_all = jnp.concatenate([c_prompt, c_sample, jnp.zeros((rpad, d), F32)], axis=0)
    mods = _ada(c_all, w_ada, b_ada)

    cache_frows = cache_fox_kv.reshape(depth, n_pool, page * fox_slots, D_HEAD)
    cache_dkv = cache_dsa_kv.reshape(depth, n_pool, page, 2 * W_DKV)

    pos = np.arange(t)
    p_hi = jnp.asarray(pos // POS_RADIX, F32)
    p_lo = jnp.asarray(pos % POS_RADIX, F32)
    slopes = jnp.asarray(2.0 ** (-8.0 * (np.arange(H_DSA) + 1) / H_DSA), F32)[None, :, None, None]
    dq_cols = [slopes * POS_RADIX, slopes, -slopes * POS_RADIX * p_hi[None, None, :, None],
               -slopes * p_lo[None, None, :, None]]
    dk_cols = [p_hi[None, None, :, None], p_lo[None, None, :, None], 1.0, 1.0]
    grp = np.arange(H_DSA) // (H_DSA // KV_DSA)

    xp = x_prompt
    xs = x_sample.reshape(1, bd, d)
    outs = [[] for _ in range(8)]
    for l in range(depth):
        mp = [mods[l, :bsz, i * d:(i + 1) * d].reshape(bsz, 1, d) for i in range(6)]
        ms = [mods[l, bsz:rows, i * d:(i + 1) * d].reshape(1, bd, d) for i in range(6)]
        w_perm = _perm_w_in(w_in[l]).astype(CD)
        bias_row = jnp.zeros((1, LANES), F32).at[0, M_FF:M_FF + H_FOX].set(b_f[l])
        gains = jnp.stack([jnp.tile(qn_fox[l], H_FOX), jnp.tile(kn_fox[l], H_FOX),
                           jnp.tile(qn_dsa[l], H_DSA), jnp.tile(kn_dsa[l], H_DSA)])
        g_a = g_attn[l].reshape(1, d)
        g_f = g_ffn[l].reshape(1, d)
        w_o = w_out[l].astype(CD)

        fkv, dkv, misc, att = _proj(xp, mp[0], mp[1], g_a, w_perm, bias_row, gains, seg, tm_w)
        logf = misc[..., M_FF:M_FF + H_FOX]
        csum = _cumsum(jnp.transpose(logf, (0, 2, 1)), u_incl)
        c1, c2, c3 = (c[..., None] for c in _split3(csum))
        fq_aug = _augment(_heads(att[..., C_FQ:C_FQ + W_FOX], H_FOX) * QK_SCALE, [-1.0, -1.0, -1.0, c1, c2, c3])
        fk_aug = _augment(_heads(att[..., C_FK:C_FK + W_FOX], H_FOX), [c1, c2, c3, 1.0, 1.0, 1.0])
        o_f = _fox_prompt(fq_aug, fk_aug, _heads(att[..., C_FV:C_FV + W_FOX], H_FOX), tm_p)
        dq_aug = _augment(_heads(att[..., C_DQ:C_DQ + W_DSA], H_DSA) * QK_SCALE, dq_cols)
        dk_aug = _augment(_heads(att[..., C_DK:C_DK + W_DKV], KV_DSA), dk_cols)
        o_d = _dsa_prompt(_heads(att[..., C_IQ:C_IQ + W_IQ], H_IDX), misc[..., M_IW:M_IW + H_IDX],
                          att[..., C_MISC + M_IK:C_MISC + M_IK + D_IDX], dq_aug, dk_aug,
                          _heads(att[..., C_DV:C_DV + W_DKV], KV_DSA), u_strict, min(DSA_TQ, kc), kc)
        xp = _merge(_unheads(o_f), _unheads(o_d), w_o, mp[2], xp, tm_p)
        outs[0].append(fkv.reshape(bsz, t, 2, H_FOX, D_HEAD))
        outs[1].append(logf)
        outs[2].append(dkv.reshape(bsz, t, 2, KV_DSA, D_HEAD))
        outs[3].append(misc[..., M_IK:M_IK + D_IDX])

        fkv_s, dkv_s, misc_s, att_s = _proj(xs, ms[0], ms[1], g_a, w_perm, bias_row, gains, seg, bd)
        logf_s = misc_s[0, :, M_FF:M_FF + H_FOX]
        q_f = att_s[0, :, C_FQ:C_FQ + W_FOX].reshape(bd, H_FOX, D_HEAD) * QK_SCALE
        o_fs = _fox_decode(page_table, q_f, fkv_s.reshape(bd, 2 * H_FOX, D_HEAD), logf_s[:, :, None],
                           cache_frows, cache_fox_logf, m_exp, ones, l, pp).reshape(1, bd, W_FOX)
        selb, self_sel = _dsa_select(page_table, att_s[0, :, C_IQ:C_IQ + W_IQ].reshape(bd, H_IDX, D_IDX),
                                     misc_s[0, :, M_IW:M_IW + H_IDX][:, :, None],
                                     misc_s[0, :, None, M_IK:M_IK + D_IDX], cache_dsa_idx, u_strict, l_strict, l, pp)
        qm_d = _block_diag_q(att_s[0, :, C_DQ:C_DQ + W_DSA] * QK_SCALE, H_DSA, KV_DSA)
        o_ds = _dsa_decode(page_table, qm_d, dkv_s.reshape(bd, 1, 2 * W_DKV), selb, self_sel, cache_dkv, l, pp)
        o_ds = o_ds.reshape(bd, H_DSA, KV_DSA, D_HEAD)[:, np.arange(H_DSA), grp].reshape(1, bd, W_DSA)
        xs = _merge(o_fs, o_ds, w_o, ms[2], xs, bd)
        outs[4].append(fkv_s.reshape(bd, 1, 2, H_FOX, D_HEAD))
        outs[5].append(logf_s.reshape(bd, 1, H_FOX))
        outs[6].append(dkv_s.reshape(bd, 1, 2, KV_DSA, D_HEAD))
        outs[7].append(misc_s[0, :, M_IK:M_IK + D_IDX].reshape(bd, 1, D_IDX))

        wq_t = jnp.transpose(peer_wq[l]).astype(CD)
        k1p = jnp.pad(peer_k1[l], ((0, 0), (0, 0), (0, D_KEY // 2)))
        k2p = jnp.pad(peer_k2[l], ((0, 0), (0, 0), (D_KEY // 2, 0)))
        kh = jnp.concatenate([k1p, k2p], axis=1)
        eye = jnp.eye(PEER_HEADS, dtype=kh.dtype)
        kbd_t = (kh[:, :, None, :] * eye[:, None, :, None]).reshape(PEER_HEADS * 2 * N_KEYS,
                                                                    PEER_HEADS * D_KEY).astype(CD)
        u_c = peer_u[l].astype(CD)
        v_c = peer_v[l].astype(CD)
        xp = _peer_b(*_peer_a(xp, mp[3], mp[4], g_f, wq_t, kbd_t, tm_w), u_c, v_c, xp, mp[5], tm_p, PEER_IC)
        xs = _peer_b(*_peer_a(xs, ms[3], ms[4], g_f, wq_t, kbd_t, bd), u_c, v_c, xs, ms[5], bd, PEER_IC)

    stk = [jnp.stack(o) for o in outs]
    return (xp, xs.reshape(bd, 1, d), stk[0], stk[1], stk[2], stk[3], stk[4], stk[5], stk[6], stk[7])
```

```python
import functools

import jax
import jax.numpy as jnp
import numpy as np
from jax import lax
from jax.experimental import pallas as pl
from jax.experimental.pallas import tpu as pltpu

D_HEAD = 64
H_FOX = 8
H_DSA = 8
KV_DSA = 2
H_IDX = 8
D_IDX = 64
TOPK_DSA = 256
PEER_HEADS = 8
N_KEYS = 128
D_KEY = 128
PEER_TOPK = 16
EPS = 1e-6

LANES = 128
F32 = jnp.float32
CD = jnp.bfloat16
VMEM_LIMIT = 48 * 1024 * 1024

W_FOX = H_FOX * D_HEAD
W_DSA = H_DSA * D_HEAD
W_DKV = KV_DSA * D_HEAD
W_IQ = H_IDX * D_IDX
C_FQ, C_FK, C_FV, C_DQ, C_IQ = 0, 512, 1024, 1536, 2048
C_DK, C_DV, C_MISC, N_PERM = 2560, 2688, 2816, 2944
M_IK, M_FF, M_IW = 0, 64, 72

NEG_INF = float("-inf")
INT_MIN = -2147483648
KEY_NEG_INF = -2139095041
POS_RADIX = 64
QK_SCALE = D_HEAD ** -0.5

TILE_ROWS = 512
PROJ_ROWS = 256
DSA_TQ = 256
DSA_KC = 512
PEER_IC = 4
PEER_SUB = 32
DEC_PAGES = 8


def _cparams(sem):
    return pltpu.CompilerParams(dimension_semantics=sem, vmem_limit_bytes=VMEM_LIMIT)


def _dot(a, b):
    return jnp.dot(a, b, preferred_element_type=F32)


def _dot_nt(a, b):
    return lax.dot_general(a, b, (((1,), (1,)), ((), ())), preferred_element_type=F32)


def _dot_tn(a, b):
    return lax.dot_general(a, b, (((0,), (0,)), ((), ())), preferred_element_type=F32)


def _dot_split(x, m, n_split, dot=_dot):
    acc = None
    r = x
    for i in range(n_split):
        p = r.astype(CD)
        d = dot(p, m)
        acc = d if acc is None else acc + d
        if i + 1 < n_split:
            r = r - p.astype(F32)
    return acc


def _split3(x):
    p1 = x.astype(CD)
    r = x - p1.astype(F32)
    p2 = r.astype(CD)
    p3 = (r - p2.astype(F32)).astype(CD)
    return p1, p2, p3


def _rms_mod(x, g, shift, scale):
    ms = jnp.mean(x * x, axis=-1, keepdims=True)
    y = x * lax.rsqrt(ms + EPS) * g
    return y * (1.0 + scale) + shift


def _softmax_step(s, v, m_ref, l_ref, acc_ref):
    m_prev = m_ref[...]
    m_new = jnp.maximum(m_prev, jnp.max(s, axis=1, keepdims=True))
    m_safe = jnp.where(m_new == NEG_INF, 0.0, m_new)
    alpha = jnp.exp(m_prev - m_safe)
    p = jnp.exp(s - m_safe)
    l_ref[...] = alpha * l_ref[...] + jnp.sum(p, axis=1, keepdims=True)
    acc_ref[...] = alpha * acc_ref[...] + _dot(p.astype(CD), v)
    m_ref[...] = m_new


def _ada_kernel(c_ref, w_ref, b_ref, o_ref):
    c = c_ref[...]
    a = (c * jax.nn.sigmoid(c)).astype(CD)
    o_ref[...] = _dot(a, w_ref[...].astype(CD)) + b_ref[...]


def _ada(c_all, w_ada, b_ada):
    depth, d, n6 = w_ada.shape
    r = c_all.shape[0]
    tn = n6 // 4
    return pl.pallas_call(
        _ada_kernel,
        grid=(depth, n6 // tn),
        in_specs=[
            pl.BlockSpec((r, d), lambda l, j: (0, 0)),
            pl.BlockSpec((None, d, tn), lambda l, j: (l, 0, j)),
            pl.BlockSpec((None, 1, tn), lambda l, j: (l, 0, j)),
        ],
        out_specs=pl.BlockSpec((None, r, tn), lambda l, j: (l, 0, j)),
        out_shape=jax.ShapeDtypeStruct((depth, r, n6), F32),
        compiler_params=_cparams(("arbitrary", "arbitrary")),
        name="ada",
    )(c_all, w_ada, b_ada.reshape(depth, 1, n6))


def _head_norm(z, gain, seg):
    sq = z * z
    ss = _dot_split(sq, seg, 2)
    return z * lax.rsqrt(ss * (1.0 / D_HEAD) + EPS) * gain


def _proj_kernel(x_ref, sh_ref, sc_ref, g_ref, w_ref, bias_ref, gains_ref, seg_ref,
                 fkv_ref, dkv_ref, misc_ref, att_ref):
    h = _rms_mod(x_ref[...], g_ref[...], sh_ref[...], sc_ref[...]).astype(CD)
    seg = seg_ref[...]

    def seg_dot(c0, width):
        return _dot(h, w_ref[:, c0:c0 + width])

    fq = _head_norm(seg_dot(C_FQ, W_FOX), gains_ref[0:1, :], seg)
    att_ref[:, C_FQ:C_FQ + W_FOX] = fq.astype(CD)
    fk = _head_norm(seg_dot(C_FK, W_FOX), gains_ref[1:2, :], seg)
    fkv_ref[:, 0:W_FOX] = fk
    att_ref[:, C_FK:C_FK + W_FOX] = fk.astype(CD)
    fv = seg_dot(C_FV, W_FOX)
    fkv_ref[:, W_FOX:2 * W_FOX] = fv
    att_ref[:, C_FV:C_FV + W_FOX] = fv.astype(CD)
    dq = _head_norm(seg_dot(C_DQ, W_DSA), gains_ref[2:3, :], seg)
    att_ref[:, C_DQ:C_DQ + W_DSA] = dq.astype(CD)
    att_ref[:, C_IQ:C_IQ + W_IQ] = seg_dot(C_IQ, W_IQ).astype(CD)
    dk = _head_norm(seg_dot(C_DK, W_DKV), gains_ref[3:4, 0:W_DKV], seg[0:W_DKV, 0:W_DKV])
    dkv_ref[:, 0:W_DKV] = dk
    att_ref[:, C_DK:C_DK + W_DKV] = dk.astype(CD)
    dv = seg_dot(C_DV, W_DKV)
    dkv_ref[:, W_DKV:2 * W_DKV] = dv
    att_ref[:, C_DV:C_DV + W_DKV] = dv.astype(CD)
    zm = seg_dot(C_MISC, LANES)
    zb = zm + bias_ref[...]
    logsig = jnp.minimum(zb, 0.0) - jnp.log1p(jnp.exp(-jnp.abs(zb)))
    lane = lax.broadcasted_iota(jnp.int32, zm.shape, 1)
    misc = jnp.where((lane >= M_FF) & (lane < M_FF + H_FOX), logsig, zm)
    misc_ref[...] = misc
    att_ref[:, C_MISC:C_MISC + LANES] = misc.astype(CD)


def _mod_spec(mod, tm, d):
    if mod.shape[1] == 1:
        return pl.BlockSpec((None, 1, d), lambda b, i: (b, 0, 0))
    return pl.BlockSpec((None, tm, d), lambda b, i: (b, i, 0))


def _proj(x, shift, scale, g, w_perm, bias_row, gains, seg, tm):
    bsz, t, d = x.shape
    row = lambda width: pl.BlockSpec((None, tm, width), lambda b, i: (b, i, 0))
    const = lambda shape: pl.BlockSpec(shape, lambda b, i: (0, 0))
    return pl.pallas_call(
        _proj_kernel,
        grid=(bsz, t // tm),
        in_specs=[row(d), _mod_spec(shift, tm, d), _mod_spec(scale, tm, d), const((1, d)),
                  const((d, N_PERM)), const((1, LANES)), const((4, W_FOX)), const((W_FOX, W_FOX))],
        out_specs=[row(2 * W_FOX), row(2 * W_DKV), row(LANES), row(N_PERM)],
        out_shape=[jax.ShapeDtypeStruct((bsz, t, 2 * W_FOX), F32),
                   jax.ShapeDtypeStruct((bsz, t, 2 * W_DKV), F32),
                   jax.ShapeDtypeStruct((bsz, t, LANES), F32),
                   jax.ShapeDtypeStruct((bsz, t, N_PERM), CD)],
        compiler_params=_cparams(("arbitrary", "arbitrary")),
        name="proj",
    )(x, shift, scale, g, w_perm, bias_row, gains, seg)


def _cumsum_kernel(x_ref, u_ref, p1_ref, p2_ref, p3_ref):
    t = x_ref.shape[-1]
    u = u_ref[...]
    carry = jnp.zeros((x_ref.shape[0], 1), F32)
    for j in range(t // LANES):
        cols = slice(j * LANES, (j + 1) * LANES)
        c = _dot_split(x_ref[:, cols], u, 3) + carry
        p1, p2, p3 = _split3(c)
        p1_ref[:, cols] = p1
        p2_ref[:, cols] = p2
        p3_ref[:, cols] = p3
        carry = c[:, LANES - 1:LANES]


def _cumsum(lf_t, u_incl):
    bsz, h, t = lf_t.shape
    spec = pl.BlockSpec((None, h, t), lambda b: (b, 0, 0))
    piece = jax.ShapeDtypeStruct((bsz, h, t), CD)
    return pl.pallas_call(
        _cumsum_kernel,
        grid=(bsz,),
        in_specs=[spec, pl.BlockSpec((LANES, LANES), lambda b: (0, 0))],
        out_specs=[spec, spec, spec],
        out_shape=[piece, piece, piece],
        compiler_params=_cparams(("arbitrary",)),
        name="cumsum",
    )(lf_t, u_incl)


def _fox_kernel(q_ref, k_ref, v_ref, o_ref, m_scr, l_scr, acc_scr, *, tq, tk):
    qi = pl.program_id(2)
    ki = pl.program_id(3)

    @pl.when(ki == 0)
    def _():
        m_scr[...] = jnp.full(m_scr.shape, NEG_INF, F32)
        l_scr[...] = jnp.zeros(l_scr.shape, F32)
        acc_scr[...] = jnp.zeros(acc_scr.shape, F32)

    def step(on_diagonal):
        s = _dot_nt(q_ref[...], k_ref[...])
        if on_diagonal:
            qpos = lax.broadcasted_iota(jnp.int32, s.shape, 0)
            kpos = lax.broadcasted_iota(jnp.int32, s.shape, 1)
            s = jnp.where(kpos <= qpos, s, NEG_INF)
        _softmax_step(s, v_ref[...], m_scr, l_scr, acc_scr)

    @pl.when(ki < qi)
    def _():
        step(False)

    @pl.when(ki == qi)
    def _():
        step(True)

    @pl.when(ki == pl.num_programs(3) - 1)
    def _():
        o_ref[...] = acc_scr[...] / l_scr[...]


def _fox_prompt(q_aug, k_aug, v, tq):
    bsz, h, t, dk = q_aug.shape
    dh = v.shape[-1]
    nq = t // tq
    kmap = lambda b, hh, i, j: (b, hh, jnp.minimum(i, j), 0)
    return pl.pallas_call(
        functools.partial(_fox_kernel, tq=tq, tk=tq),
        grid=(bsz, h, nq, nq),
        in_specs=[pl.BlockSpec((None, None, tq, dk), lambda b, hh, i, j: (b, hh, i, 0)),
                  pl.BlockSpec((None, None, tq, dk), kmap),
                  pl.BlockSpec((None, None, tq, dh), kmap)],
        out_specs=pl.BlockSpec((None, None, tq, dh), lambda b, hh, i, j: (b, hh, i, 0)),
        out_shape=jax.ShapeDtypeStruct((bsz, h, t, dh), F32),
        scratch_shapes=[pltpu.VMEM((tq, 1), F32), pltpu.VMEM((tq, 1), F32), pltpu.VMEM((tq, dh), F32)],
        compiler_params=_cparams(("arbitrary",) * 4),
        name="fox_prompt",
    )(q_aug, k_aug, v)


def _sort_key(x):
    b = pltpu.bitcast(x, jnp.int32)
    return b ^ (lax.shift_right_arithmetic(b, 31) & jnp.int32(0x7FFFFFFF))


def _kth_largest_key(count_ge, shape, k):
    def body(i, res):
        cand = res + lax.shift_left(jnp.int32(1), jnp.int32(31) - i)
        return jnp.where(count_ge(cand) >= k, cand, res)

    return lax.fori_loop(0, 32, body, jnp.full(shape, INT_MIN, jnp.int32))


def _count(mask):
    return jnp.sum(jnp.where(mask, 1.0, 0.0), axis=1, keepdims=True)


def _lane_fold(x, op):
    return functools.reduce(op, [x[:, j * LANES:(j + 1) * LANES] for j in range(x.shape[1] // LANES)])


def _dsa_kernel(iq_ref, iw_ref, ik_ref, dq_ref, dk_ref, dv_ref, us_ref, o_ref,
                key_scr, selb_scr, m_scr, l_scr, acc_scr, *, tq, kc, k_sel):
    qi = pl.program_id(1)
    n_c = (qi * tq + tq + kc - 1) // kc
    w = iw_ref[...] * (H_IDX ** -0.5) * (D_IDX ** -0.5)
    qpos = qi * tq + lax.broadcasted_iota(jnp.int32, (tq, kc), 0)
    kk = float(k_sel)

    def score_body(c, carry):
        k0 = pl.multiple_of(c * kc, kc)
        ikc = ik_ref[pl.ds(k0, kc), :]
        score = jnp.zeros((tq, kc), F32)
        for h in range(H_IDX):
            score = score + w[:, h:h + 1] * jnp.maximum(_dot_nt(iq_ref[h], ikc), 0.0)
        kpos = k0 + lax.broadcasted_iota(jnp.int32, (tq, kc), 1)
        key_scr[c] = _sort_key(jnp.where(kpos <= qpos, score, NEG_INF))
        return carry

    lax.fori_loop(0, n_c, score_body, 0)

    def chunk_count(pred):
        part = lax.fori_loop(
            0, n_c, lambda c, a: a + _lane_fold(jnp.where(pred(key_scr[c]), 1.0, 0.0), jnp.add),
            jnp.zeros((tq, LANES), F32))
        return jnp.sum(part, axis=1, keepdims=True)

    thr = _kth_largest_key(lambda cand: chunk_count(lambda key: key >= cand), (tq, 1), kk)
    is_gt = lambda key: (key > thr) & (key != KEY_NEG_INF)
    is_eq = lambda key: (key == thr) & (key != KEY_NEG_INF)
    need = kk - chunk_count(is_gt)
    n_eq = chunk_count(is_eq)

    def selb_body(c, carry):
        key = key_scr[c]
        selb_scr[c] = jnp.where(is_gt(key) | is_eq(key), 0.0, NEG_INF)
        return carry

    lax.fori_loop(0, n_c, selb_body, 0)

    @pl.when(jnp.max(n_eq - need) > 0.0)
    def _():
        us = us_ref[...]

        def tie_body(c, before):
            key = key_scr[c]
            eq = is_eq(key)
            eqf = jnp.where(eq, 1.0, 0.0).astype(CD)
            ranks = []
            for j in range(kc // LANES):
                blk = eqf[:, j * LANES:(j + 1) * LANES]
                ranks.append(_dot(blk, us) + before)
                before = before + jnp.sum(blk.astype(F32), axis=1, keepdims=True)
            rank = jnp.concatenate(ranks, axis=1)
            selb_scr[c] = jnp.where(is_gt(key) | (eq & (rank < need)), 0.0, NEG_INF)
            return before

        lax.fori_loop(0, n_c, tie_body, jnp.zeros((tq, 1), F32))

    m_scr[...] = jnp.full(m_scr.shape, NEG_INF, F32)
    l_scr[...] = jnp.zeros(l_scr.shape, F32)
    acc_scr[...] = jnp.zeros(acc_scr.shape, F32)
    rep = H_DSA // KV_DSA

    def logits(c, h):
        k0 = pl.multiple_of(c * kc, kc)
        return _dot_nt(dq_ref[h], dk_ref[h // rep, pl.ds(k0, kc), :]) + selb_scr[c]

    def max_body(c, carry):
        for h in range(H_DSA):
            m_scr[h] = jnp.maximum(m_scr[h], _lane_fold(logits(c, h), jnp.maximum))
        return carry

    lax.fori_loop(0, n_c, max_body, 0)
    for h in range(H_DSA):
        m_scr[h] = jnp.broadcast_to(jnp.max(m_scr[h], axis=1, keepdims=True), (tq, LANES))

    def att_body(c, carry):
        k0 = pl.multiple_of(c * kc, kc)
        for h in range(H_DSA):
            s = logits(c, h)
            m = m_scr[h]
            p = [jnp.exp(s[:, j * LANES:(j + 1) * LANES] - m) for j in range(kc // LANES)]
            l_scr[h] = l_scr[h] + functools.reduce(jnp.add, p)
            acc_scr[h] = acc_scr[h] + _dot(jnp.concatenate(p, axis=1).astype(CD),
                                           dv_ref[h // rep, pl.ds(k0, kc), :])
        return carry

    lax.fori_loop(0, n_c, att_body, 0)
    for h in range(H_DSA):
        o_ref[h] = acc_scr[h] / jnp.sum(l_scr[h], axis=1, keepdims=True)


def _dsa_prompt(iq, iw, ik, dq_aug, dk_aug, dv, u_strict, tq, kc):
    bsz, hi, t, di = iq.shape
    dka = dq_aug.shape[-1]
    k_sel = min(TOPK_DSA, t // 4)
    qh = lambda nh, w: pl.BlockSpec((None, nh, tq, w), lambda b, i: (b, 0, i, 0))
    full = lambda nh, w: pl.BlockSpec((None, nh, t, w), lambda b, i: (b, 0, 0, 0))
    return pl.pallas_call(
        functools.partial(_dsa_kernel, tq=tq, kc=kc, k_sel=k_sel),
        grid=(bsz, t // tq),
        in_specs=[qh(hi, di),
                  pl.BlockSpec((None, tq, H_IDX), lambda b, i: (b, i, 0)),
                  pl.BlockSpec((None, t, di), lambda b, i: (b, 0, 0)),
                  qh(H_DSA, dka), full(KV_DSA, dka), full(KV_DSA, D_HEAD),
                  pl.BlockSpec((LANES, LANES), lambda b, i: (0, 0))],
        out_specs=qh(H_DSA, D_HEAD),
        out_shape=jax.ShapeDtypeStruct((bsz, H_DSA, t, D_HEAD), F32),
        scratch_shapes=[pltpu.VMEM((t // kc, tq, kc), jnp.int32), pltpu.VMEM((t // kc, tq, kc), F32),
                        pltpu.VMEM((H_DSA, tq, LANES), F32), pltpu.VMEM((H_DSA, tq, LANES), F32),
                        pltpu.VMEM((H_DSA, tq, D_HEAD), F32)],
        compiler_params=_cparams(("arbitrary", "arbitrary")),
        name="dsa_prompt",
    )(iq, iw, ik, dq_aug, dk_aug, dv, u_strict)


def _merge_kernel(of_ref, od_ref, w_ref, gate_ref, x_ref, o_ref):
    half = of_ref.shape[-1]
    y = _dot(of_ref[...].astype(CD), w_ref[0:half, :]) + _dot(od_ref[...].astype(CD), w_ref[half:, :])
    o_ref[...] = x_ref[...] + gate_ref[...] * y


def _merge(o_f, o_d, w_out, gate, x, tm):
    bsz, t, d = x.shape
    half = o_f.shape[-1]
    row = lambda width: pl.BlockSpec((None, tm, width), lambda b, i: (b, i, 0))
    return pl.pallas_call(
        _merge_kernel,
        grid=(bsz, t // tm),
        in_specs=[row(half), row(half), pl.BlockSpec((2 * half, d), lambda b, i: (0, 0)),
                  _mod_spec(gate, tm, d), row(d)],
        out_specs=row(d),
        out_shape=jax.ShapeDtypeStruct((bsz, t, d), F32),
        compiler_params=_cparams(("arbitrary", "arbitrary")),
        name="merge",
    )(o_f, o_d, w_out, gate, x)


def _top_rows(x, n):
    vals = []
    for _ in range(n):
        m = jnp.max(x, axis=0, keepdims=True)
        vals.append(m)
        x = jnp.where(x == m, NEG_INF, x)
    return vals


_PEER_PAIRS = [(i, j) for i in range(PEER_TOPK) for j in range(PEER_TOPK) if (i + 1) * (j + 1) <= PEER_TOPK]
_PEER_CAND_ROWS = -(-len(_PEER_PAIRS) // 8) * 8


def _peer_a_kernel(x_ref, sh_ref, sc_ref, g_ref, wqt_ref, kbdt_ref, hq_ref, cut_ref, e1_ref, s2_ref, e2_ref,
                   cand_scr, candw_scr):
    hq = _rms_mod(x_ref[...], g_ref[...], sh_ref[...], sc_ref[...]).astype(CD)
    hq_ref[...] = hq
    qt = _dot_nt(wqt_ref[...], hq)
    st = _dot(kbdt_ref[...], qt.astype(CD))
    cand_scr[...] = jnp.full(cand_scr.shape, NEG_INF, F32)
    candw_scr[...] = jnp.zeros(candw_scr.shape, F32)
    for h in range(PEER_HEADS):
        r0 = h * 2 * N_KEYS
        s1 = st[r0:r0 + N_KEYS]
        s2 = st[r0 + N_KEYS:r0 + 2 * N_KEYS]
        v1 = _top_rows(s1, PEER_TOPK)
        v2 = _top_rows(s2, PEER_TOPK)
        w1 = [jnp.exp(a - v1[0]) for a in v1]
        w2 = [jnp.exp(a - v2[0]) for a in v2]
        sums = {}
        for r, (i, j) in enumerate(_PEER_PAIRS):
            sums[i, j] = v1[i] + v2[j]
            cand_scr[r:r + 1, :] = sums[i, j]
            candw_scr[r:r + 1, :] = w1[i] * w2[j]
        cand = cand_scr[...]
        tau = _top_rows(cand, PEER_TOPK)[-1]
        z = jnp.sum(jnp.where(cand >= tau, candw_scr[...], 0.0), axis=0, keepdims=True)
        cut = jnp.full(s1.shape, jnp.inf, F32)
        for i in range(PEER_TOPK):
            cut_i = jnp.full(tau.shape, jnp.inf, F32)
            for j in range(PEER_TOPK // (i + 1)):
                cut_i = jnp.where(sums[i, j] >= tau, v2[j], cut_i)
            cut = jnp.where(s1 == v1[i], cut_i, cut)
        h0 = h * N_KEYS
        lbw = cut_ref.shape[-1]
        e1 = jnp.exp(s1 - v1[0])
        e2 = jnp.exp(s2 - v2[0]) / z
        for j in range(cut_ref.shape[0]):
            cols = slice(j * lbw, (j + 1) * lbw)
            cut_ref[j, h0:h0 + N_KEYS, :] = cut[:, cols]
            e1_ref[j, h0:h0 + N_KEYS, :] = e1[:, cols]
            s2_ref[j, h0:h0 + N_KEYS, :] = s2[:, cols]
            e2_ref[j, h0:h0 + N_KEYS, :] = e2[:, cols]


def _peer_a(x, shift, scale, g, wq_t, kbd_t, tm):
    bsz, t, d = x.shape
    ns = PEER_HEADS * N_KEYS
    lbw = min(LANES, tm)
    row = pl.BlockSpec((None, tm, d), lambda b, i: (b, i, 0))
    col = pl.BlockSpec((None, tm // lbw, ns, lbw), lambda b, i: (b, i, 0, 0))
    const = lambda shape: pl.BlockSpec(shape, lambda b, i: (0, 0))
    cols = jax.ShapeDtypeStruct((bsz, t // lbw, ns, lbw), F32)
    return pl.pallas_call(
        _peer_a_kernel,
        grid=(bsz, t // tm),
        in_specs=[row, _mod_spec(shift, tm, d), _mod_spec(scale, tm, d), const((1, d)),
                  const(wq_t.shape), const(kbd_t.shape)],
        out_specs=[row, col, col, col, col],
        out_shape=[jax.ShapeDtypeStruct((bsz, t, d), CD), cols, cols, cols, cols],
        scratch_shapes=[pltpu.VMEM((_PEER_CAND_ROWS, tm), F32), pltpu.VMEM((_PEER_CAND_ROWS, tm), F32)],
        compiler_params=_cparams(("arbitrary", "arbitrary")),
        name="peer_a",
    )(x, shift, scale, g, wq_t, kbd_t)


def _peer_b_kernel(hq_ref, cut_ref, e1_ref, s2_ref, e2_ref, u_ref, v_ref, x_ref, gate_ref, o_ref,
                   acc_scr, coef_scr, gel_scr, *, ic, sub):
    c = pl.program_id(2)
    nlb, _, lbw = cut_ref.shape

    @pl.when(c == 0)
    def _():
        acc_scr[...] = jnp.zeros(acc_scr.shape, F32)

    act = _dot_nt(u_ref[...], hq_ref[...])
    for j in range(nlb):
        a = act[:, j * lbw:(j + 1) * lbw]
        gel_scr[j] = 0.5 * a * (1.0 + lax.erf(a * (2.0 ** -0.5)))

    n_r = N_KEYS // sub

    def tile_body(t, carry):
        j = t // n_r
        r = pl.multiple_of((t % n_r) * sub, sub)
        gates = [jnp.zeros((sub, lbw), F32) for _ in range(ic)]
        for h in range(PEER_HEADS):
            s2 = s2_ref[j, pl.ds(h * N_KEYS + r, sub), :]
            e2 = e2_ref[j, pl.ds(h * N_KEYS + r, sub), :]
            for ii in range(ic):
                cut = cut_ref[j, pl.ds(h * N_KEYS + c * ic + ii, 1), :]
                e1 = e1_ref[j, pl.ds(h * N_KEYS + c * ic + ii, 1), :]
                gates[ii] = gates[ii] + jnp.where(s2 >= cut, e1 * e2, 0.0)
        for ii in range(ic):
            rows = pl.ds(ii * N_KEYS + r, sub)
            coef_scr[j, rows, :] = (gates[ii] * gel_scr[j, rows, :]).astype(CD)
        return carry

    lax.fori_loop(0, nlb * n_r, tile_body, 0)
    coef = jnp.concatenate([coef_scr[j] for j in range(nlb)], axis=1)
    acc_scr[...] += _dot_tn(coef, v_ref[...])

    @pl.when(c == pl.num_programs(2) - 1)
    def _():
        o_ref[...] = x_ref[...] + gate_ref[...] * acc_scr[...]


def _peer_b(hq, cut, e1, s2, e2, u_c, v_c, x, gate, tm, ic):
    bsz, t, d = x.shape
    _, _, ns, lbw = cut.shape
    nlb = tm // lbw
    ec = ic * N_KEYS
    row = pl.BlockSpec((None, tm, d), lambda b, i, c: (b, i, 0))
    col = pl.BlockSpec((None, nlb, ns, lbw), lambda b, i, c: (b, i, 0, 0))
    wspec = pl.BlockSpec((ec, d), lambda b, i, c: (c, 0))
    if gate.shape[1] == 1:
        gspec = pl.BlockSpec((None, 1, d), lambda b, i, c: (b, 0, 0))
    else:
        gspec = row
    return pl.pallas_call(
        functools.partial(_peer_b_kernel, ic=ic, sub=PEER_SUB),
        grid=(bsz, t // tm, N_KEYS // ic),
        in_specs=[row, col, col, col, col, wspec, wspec, row, gspec],
        out_specs=row,
        out_shape=jax.ShapeDtypeStruct((bsz, t, d), F32),
        scratch_shapes=[pltpu.VMEM((tm, d), F32), pltpu.VMEM((nlb, ec, lbw), CD),
                        pltpu.VMEM((nlb, ec, lbw), F32)],
        compiler_params=_cparams(("arbitrary",) * 3),
        name="peer_b",
    )(hq, cut, e1, s2, e2, u_c, v_c, x, gate)


def _qk_rows(qb_ref, kt_of_head, n_heads):
    return jnp.concatenate([jnp.sum(qb_ref[h] * kt_of_head(h), axis=0, keepdims=True) for h in range(n_heads)],
                           axis=0)


def _pv_update(acc_scr, alpha, probs, vt_of):
    for h in range(acc_scr.shape[0]):
        a = alpha[h:h + 1, :] * acc_scr[h]
        for p, prob in enumerate(probs):
            a = a + prob[h:h + 1, :] * vt_of(p, h)
        acc_scr[h] = a


def _fox_dec_kernel(pt_ref, q_ref, qb_ref, kn_ref, vcol_ref, lfn_ref, msuf_ref, *rest, pp):
    kv_refs = rest[:pp]
    lf_refs = rest[pp:2 * pp]
    o_ref, m_scr, l_scr, acc_scr, carry_scr = rest[2 * pp:]
    j = pl.program_id(1)

    @pl.when(j == 0)
    def _():
        m_scr[...] = jnp.sum(q_ref[...] * kn_ref[...], axis=1, keepdims=True)
        l_scr[...] = jnp.ones(l_scr.shape, F32)
        acc_scr[...] = vcol_ref[...]
        carry_scr[...] = lfn_ref[...]

    carry = carry_scr[...]
    logits = []
    for p in range(pp):
        lft = lf_refs[p][...]
        decay = _dot_split(lft, msuf_ref[...], 3) + carry
        logits.append(_qk_rows(qb_ref, lambda h: kv_refs[p][0, h], H_FOX) + decay)
        carry = carry + jnp.sum(lft, axis=1, keepdims=True)
    carry_scr[...] = carry
    top = functools.reduce(jnp.maximum, logits)
    m_prev = m_scr[...]
    m_new = jnp.maximum(m_prev, jnp.max(top, axis=1, keepdims=True))
    alpha = jnp.exp(m_prev - m_new)
    probs = [jnp.exp(s - m_new) for s in logits]
    _pv_update(acc_scr, alpha, probs, lambda p, h: kv_refs[p][1, h])
    l_scr[...] = alpha * l_scr[...] + jnp.sum(functools.reduce(jnp.add, probs), axis=1, keepdims=True)
    m_scr[...] = m_new

    @pl.when(j == pl.num_programs(1) - 1)
    def _():
        for h in range(H_FOX):
            o_ref[h] = jnp.sum(acc_scr[h], axis=1, keepdims=True) / l_scr[h:h + 1, :]


def _fox_decode(page_table, q, qb, k_new, v_col, lf_new, cache_t, cache_lft, m_suf, layer, pp):
    bd, n_pages = page_table.shape
    page = cache_t.shape[-1]
    per_b = lambda shape: pl.BlockSpec((None,) + shape, lambda b, j, pt: (b,) + (0,) * len(shape))

    def page_spec(shape, p):
        return pl.BlockSpec((None, None) + shape,
                            lambda b, j, pt: (layer, pt[b, n_pages - 1 - (j * pp + p)]) + (0,) * len(shape))

    grid_spec = pltpu.PrefetchScalarGridSpec(
        num_scalar_prefetch=1,
        grid=(bd, n_pages // pp),
        in_specs=[per_b((H_FOX, D_HEAD)), per_b((H_FOX, D_HEAD, page)), per_b((H_FOX, D_HEAD)),
                  per_b((H_FOX, D_HEAD, page)), per_b((H_FOX, 1)),
                  pl.BlockSpec((page, page), lambda b, j, pt: (0, 0))]
                 + [page_spec((2, H_FOX, D_HEAD, page), p) for p in range(pp)]
                 + [page_spec((H_FOX, page), p) for p in range(pp)],
        out_specs=per_b((H_FOX, D_HEAD, 1)),
        scratch_shapes=[pltpu.VMEM((H_FOX, 1), F32), pltpu.VMEM((H_FOX, 1), F32),
                        pltpu.VMEM((H_FOX, D_HEAD, page), F32), pltpu.VMEM((H_FOX, 1), F32)],
    )
    return pl.pallas_call(
        functools.partial(_fox_dec_kernel, pp=pp),
        grid_spec=grid_spec,
        out_shape=jax.ShapeDtypeStruct((bd, H_FOX, D_HEAD, 1), F32),
        compiler_params=_cparams(("arbitrary", "arbitrary")),
        name="fox_decode",
    )(page_table, q, qb, k_new, v_col, lf_new, m_suf, *([cache_t] * pp), *([cache_lft] * pp))


def _dsa_sel_kernel(pt_ref, iq_ref, iw_ref, ikn_ref, us_ref, ls_ref, *rest, pp, n_pages, k_sel):
    ik_refs = rest[:pp]
    selb_ref, self_ref, sc_scr = rest[pp:]
    j = pl.program_id(1)
    page = ik_refs[0].shape[1]
    w = iw_ref[...] * (H_IDX ** -0.5) * (D_IDX ** -0.5)
    iq = iq_ref[...]
    for p in range(pp):
        r = jnp.maximum(_dot(iq, ik_refs[p][...].astype(CD)), 0.0)
        sc_scr[pl.ds(j * pp + p, 1), :] = jnp.sum(w * r, axis=0, keepdims=True)

    @pl.when(j == pl.num_programs(1) - 1)
    def _():
        ikn = ikn_ref[...].astype(CD).astype(F32)
        r_self = jnp.maximum(jnp.sum(iq.astype(F32) * ikn, axis=1, keepdims=True), 0.0)
        s_self = jnp.sum(w * r_self, axis=0, keepdims=True)
        key = _sort_key(sc_scr[...])
        key_self = _sort_key(s_self)
        total = lambda m: jnp.sum(_count(m), axis=0, keepdims=True)

        def count_ge(cand):
            return total(key >= cand) + jnp.where(key_self >= cand, 1.0, 0.0)

        thr = _kth_largest_key(count_ge, (1, 1), float(k_sel))
        gt = key > thr
        eq = key == thr
        eqf = jnp.where(eq, 1.0, 0.0)
        need = float(k_sel) - (total(gt) + jnp.where(key_self > thr, 1.0, 0.0))
        row_eq = jnp.sum(eqf, axis=1, keepdims=True)
        before = _dot(ls_ref[...], jnp.broadcast_to(row_eq, (n_pages, page)).astype(CD))
        rank = _dot(eqf.astype(CD), us_ref[...]) + before
        selb_ref[...] = jnp.where(gt | (eq & (rank < need)), 0.0, NEG_INF)
        n_eq_past = jnp.sum(row_eq, axis=0, keepdims=True)
        self_sel = (key_self > thr) | ((key_self == thr) & (n_eq_past < need))
        self_ref[...] = jnp.broadcast_to(jnp.where(self_sel, 1.0, 0.0), self_ref.shape)


def _dsa_select(page_table, iq, iw_col, ik_new, cache_idx, u_strict, l_strict, layer, pp):
    bd, n_pages = page_table.shape
    page = cache_idx.shape[-1]
    k_sel = min(TOPK_DSA, (n_pages * page + 1) // 4)
    per_b = lambda shape: pl.BlockSpec((None,) + shape, lambda b, j, pt: (b, 0, 0))
    const = lambda shape: pl.BlockSpec(shape, lambda b, j, pt: (0, 0))
    grid_spec = pltpu.PrefetchScalarGridSpec(
        num_scalar_prefetch=1,
        grid=(bd, n_pages // pp),
        in_specs=[per_b((H_IDX, D_IDX)), per_b((H_IDX, 1)), per_b((1, D_IDX)),
                  const((page, page)), const((n_pages, n_pages))]
                 + [pl.BlockSpec((None, None, D_IDX, page),
                                 lambda b, j, pt, p=p: (layer, pt[b, j * pp + p], 0, 0)) for p in range(pp)],
        out_specs=[per_b((n_pages, page)), per_b((H_DSA, LANES))],
        scratch_shapes=[pltpu.VMEM((n_pages, page), F32)],
    )
    return pl.pallas_call(
        functools.partial(_dsa_sel_kernel, pp=pp, n_pages=n_pages, k_sel=k_sel),
        grid_spec=grid_spec,
        out_shape=[jax.ShapeDtypeStruct((bd, n_pages, page), F32),
                   jax.ShapeDtypeStruct((bd, H_DSA, LANES), F32)],
        compiler_params=_cparams(("arbitrary", "arbitrary")),
        name="dsa_select",
    )(page_table, iq, iw_col, ik_new, u_strict, l_strict, *([cache_idx] * pp))


def _dsa_dec_kernel(pt_ref, q_ref, qb_ref, kn_ref, vcol_ref, selb_ref, self_ref, *rest, pp, n_pages):
    kv_refs = rest[:pp]
    o_ref, m_scr, l_scr, acc_scr = rest[pp:]
    j = pl.program_id(1)
    page = kv_refs[0].shape[-1]
    rep = H_DSA // KV_DSA

    @pl.when(j == 0)
    def _():
        self_sel = self_ref[:, 0:1] > 0.0
        s_new = jnp.sum(q_ref[...] * kn_ref[...], axis=1, keepdims=True)
        m_scr[...] = jnp.where(self_sel, s_new, NEG_INF)
        l_scr[...] = jnp.where(self_sel, 1.0, 0.0)
        for h in range(H_DSA):
            acc_scr[h] = jnp.where(self_ref[h:h + 1, 0:1] > 0.0, vcol_ref[h], 0.0)

    hh = lax.broadcasted_iota(jnp.int32, (H_DSA, page), 0)
    slope = jnp.exp2(-8.0 * (hh + 1).astype(F32) / H_DSA)
    lane = lax.broadcasted_iota(jnp.int32, (H_DSA, page), 1)
    logits = []
    for p in range(pp):
        pg = j * pp + p
        dist = (n_pages * page - (pg * page + lane)).astype(F32)
        s = _qk_rows(qb_ref, lambda h: kv_refs[p][0, h // rep], H_DSA)
        logits.append(s - slope * dist + selb_ref[pl.ds(pg, 1), :])
    top = functools.reduce(jnp.maximum, logits)
    m_prev = m_scr[...]
    m_new = jnp.maximum(m_prev, jnp.max(top, axis=1, keepdims=True))
    m_safe = jnp.where(m_new == NEG_INF, 0.0, m_new)
    alpha = jnp.exp(m_prev - m_safe)
    probs = [jnp.exp(s - m_safe) for s in logits]
    _pv_update(acc_scr, alpha, probs, lambda p, h: kv_refs[p][1, h // rep])
    l_scr[...] = alpha * l_scr[...] + jnp.sum(functools.reduce(jnp.add, probs), axis=1, keepdims=True)
    m_scr[...] = m_new

    @pl.when(j == pl.num_programs(1) - 1)
    def _():
        for h in range(H_DSA):
            o_ref[h] = jnp.sum(acc_scr[h], axis=1, keepdims=True) / l_scr[h:h + 1, :]


def _dsa_decode(page_table, q, qb, k_new, v_col, selb, self_sel, cache_t, layer, pp):
    bd, n_pages = page_table.shape
    page = cache_t.shape[-1]
    per_b = lambda shape: pl.BlockSpec((None,) + shape, lambda b, j, pt: (b,) + (0,) * len(shape))
    grid_spec = pltpu.PrefetchScalarGridSpec(
        num_scalar_prefetch=1,
        grid=(bd, n_pages // pp),
        in_specs=[per_b((H_DSA, D_HEAD)), per_b((H_DSA, D_HEAD, page)), per_b((H_DSA, D_HEAD)),
                  per_b((H_DSA, D_HEAD, page)), per_b((n_pages, page)), per_b((H_DSA, LANES))]
                 + [pl.BlockSpec((None, None, 2, KV_DSA, D_HEAD, page),
                                 lambda b, j, pt, p=p: (layer, pt[b, j * pp + p], 0, 0, 0, 0)) for p in range(pp)],
        out_specs=per_b((H_DSA, D_HEAD, 1)),
        scratch_shapes=[pltpu.VMEM((H_DSA, 1), F32), pltpu.VMEM((H_DSA, 1), F32),
                        pltpu.VMEM((H_DSA, D_HEAD, page), F32)],
    )
    return pl.pallas_call(
        functools.partial(_dsa_dec_kernel, pp=pp, n_pages=n_pages),
        grid_spec=grid_spec,
        out_shape=jax.ShapeDtypeStruct((bd, H_DSA, D_HEAD, 1), F32),
        compiler_params=_cparams(("arbitrary", "arbitrary")),
        name="dsa_decode",
    )(page_table, q, qb, k_new, v_col, selb, self_sel, *([cache_t] * pp))


def _perm_w_in(w):
    o = 0
    segs = {}
    for name, width in (("fq", W_FOX), ("fk", W_FOX), ("fv", W_FOX), ("ff", H_FOX), ("dq", W_DSA),
                        ("dk", W_DKV), ("dv", W_DKV), ("iq", W_IQ), ("ik", D_IDX), ("iw", H_IDX)):
        segs[name] = w[:, o:o + width]
        o += width
    pad = jnp.zeros((w.shape[0], LANES - D_IDX - H_FOX - H_IDX), w.dtype)
    return jnp.concatenate([segs["fq"], segs["fk"], segs["fv"], segs["dq"], segs["iq"], segs["dk"], segs["dv"],
                            segs["ik"], segs["ff"], segs["iw"], pad], axis=1)


def _heads(a, n):
    b, t, _ = a.shape
    return jnp.transpose(a.reshape(b, t, n, D_HEAD), (0, 2, 1, 3))


def _unheads(a):
    b, n, t, dh = a.shape
    return jnp.transpose(a, (0, 2, 1, 3)).reshape(b, t, n * dh)


def _augment(base, cols):
    shape = base.shape[:-1]
    cols = [jnp.broadcast_to(jnp.asarray(c, CD), shape + (1,)) for c in cols]
    pad = jnp.zeros(shape + (2 * D_HEAD - base.shape[-1] - len(cols),), CD)
    return jnp.concatenate([base.astype(CD)] + cols + [pad], axis=-1)


def _decode_operands(q, k_new, v_new, page):
    q = (q * QK_SCALE).astype(F32)
    qb = jnp.broadcast_to(q[..., None], q.shape + (page,))
    lane0 = (jnp.arange(page) == 0).astype(F32)
    return q, qb, k_new, v_new[..., None] * lane0


def _tile(n, pref):
    for t in (pref, pref // 2, pref // 4):
        if t >= 8 and n % t == 0:
            return t
    return n


def kernel(x_prompt, x_sample, c_prompt, c_sample, cache_fox_kv, cache_fox_logf, cache_dsa_kv, cache_dsa_idx,
           page_table, w_ada, b_ada, g_attn, g_ffn, w_in, b_f, qn_fox, kn_fox, qn_dsa, kn_dsa, w_out,
           peer_wq, peer_k1, peer_k2, peer_u, peer_v):
    bsz, t, d = x_prompt.shape
    bd, ts, _ = x_sample.shape
    depth = w_in.shape[0]
    n_pool, page = cache_fox_kv.shape[1], cache_fox_kv.shape[2]
    n_pages = page_table.shape[1]
    assert ts == 1 and t % LANES == 0 and t < POS_RADIX * 256 and page == LANES
    tm_p = _tile(t, TILE_ROWS)
    tm_w = _tile(t, PROJ_ROWS)
    kc = _tile(t, DSA_KC)
    pp = _tile(n_pages, DEC_PAGES) if n_pages >= 8 else n_pages

    ii = np.arange(LANES)
    u_incl = jnp.asarray(ii[:, None] <= ii[None, :], CD)
    u_strict = jnp.asarray(ii[:, None] < ii[None, :], CD)
    pg = np.arange(n_pages)
    l_strict = jnp.asarray(pg[None, :] < pg[:, None], CD)
    hh = np.arange(W_FOX) // D_HEAD
    seg = jnp.asarray(hh[:, None] == hh[None, :], CD)
    m_suf = jnp.asarray(ii[:, None] > ii[None, :], CD)

    rows = bsz + bd
    rpad = -rows % 8
    c_all = jnp.concatenate([c_prompt, c_sample, jnp.zeros((rpad, d), F32)], axis=0)
    mods = _ada(c_all, w_ada, b_ada)

    cache_fkv_t = jnp.transpose(cache_fox_kv, (0, 1, 3, 4, 5, 2))
    cache_lf_t = jnp.transpose(cache_fox_logf, (0, 1, 3, 2))
    cache_dkv_t = jnp.transpose(cache_dsa_kv, (0, 1, 3, 4, 5, 2))
    cache_idx_t = jnp.transpose(cache_dsa_idx, (0, 1, 3, 2))

    pos = np.arange(t)
    p_hi = jnp.asarray(pos // POS_RADIX, F32)
    p_lo = jnp.asarray(pos % POS_RADIX, F32)
    slopes = jnp.asarray(2.0 ** (-8.0 * (np.arange(H_DSA) + 1) / H_DSA), F32)[None, :, None, None]
    dq_cols = [slopes * POS_RADIX, slopes, -slopes * POS_RADIX * p_hi[None, None, :, None],
               -slopes * p_lo[None, None, :, None]]
    dk_cols = [p_hi[None, None, :, None], p_lo[None, None, :, None], 1.0, 1.0]
    grp = np.arange(H_DSA) // (H_DSA // KV_DSA)

    xp = x_prompt
    xs = x_sample.reshape(1, bd, d)
    outs = [[] for _ in range(8)]
    for l in range(depth):
        mp = [mods[l, :bsz, i * d:(i + 1) * d].reshape(bsz, 1, d) for i in range(6)]
        ms = [mods[l, bsz:rows, i * d:(i + 1) * d].reshape(1, bd, d) for i in range(6)]
        w_perm = _perm_w_in(w_in[l]).astype(CD)
        bias_row = jnp.zeros((1, LANES), F32).at[0, M_FF:M_FF + H_FOX].set(b_f[l])
        gains = jnp.stack([jnp.tile(qn_fox[l], H_FOX), jnp.tile(kn_fox[l], H_FOX),
                           jnp.tile(qn_dsa[l], H_DSA), jnp.tile(kn_dsa[l], H_DSA)])
        g_a = g_attn[l].reshape(1, d)
        g_f = g_ffn[l].reshape(1, d)
        w_o = w_out[l].astype(CD)

        fkv, dkv, misc, att = _proj(xp, mp[0], mp[1], g_a, w_perm, bias_row, gains, seg, tm_w)
        logf = misc[..., M_FF:M_FF + H_FOX]
        c1, c2, c3 = (c[..., None] for c in _cumsum(jnp.transpose(logf, (0, 2, 1)), u_incl))
        fq_aug = _augment(_heads(att[..., C_FQ:C_FQ + W_FOX], H_FOX) * QK_SCALE, [-1.0, -1.0, -1.0, c1, c2, c3])
        fk_aug = _augment(_heads(att[..., C_FK:C_FK + W_FOX], H_FOX), [c1, c2, c3, 1.0, 1.0, 1.0])
        o_f = _fox_prompt(fq_aug, fk_aug, _heads(att[..., C_FV:C_FV + W_FOX], H_FOX), tm_p)
        dq_aug = _augment(_heads(att[..., C_DQ:C_DQ + W_DSA], H_DSA) * QK_SCALE, dq_cols)
        dk_aug = _augment(_heads(att[..., C_DK:C_DK + W_DKV], KV_DSA), dk_cols)
        o_d = _dsa_prompt(_heads(att[..., C_IQ:C_IQ + W_IQ], H_IDX), misc[..., M_IW:M_IW + H_IDX],
                          att[..., C_MISC + M_IK:C_MISC + M_IK + D_IDX], dq_aug, dk_aug,
                          _heads(att[..., C_DV:C_DV + W_DKV], KV_DSA), u_strict, min(DSA_TQ, kc), kc)
        xp = _merge(_unheads(o_f), _unheads(o_d), w_o, mp[2], xp, tm_p)
        outs[0].append(fkv.reshape(bsz, t, 2, H_FOX, D_HEAD))
        outs[1].append(logf)
        outs[2].append(dkv.reshape(bsz, t, 2, KV_DSA, D_HEAD))
        outs[3].append(misc[..., M_IK:M_IK + D_IDX])

        fkv_s, dkv_s, misc_s, att_s = _proj(xs, ms[0], ms[1], g_a, w_perm, bias_row, gains, seg, bd)
        logf_s = misc_s[0, :, M_FF:M_FF + H_FOX]
        fkv_new = fkv_s.reshape(bd, 2, H_FOX, D_HEAD)
        o_fs = _fox_decode(page_table, *_decode_operands(att_s[0, :, C_FQ:C_FQ + W_FOX].reshape(bd, H_FOX, D_HEAD),
                                                         fkv_new[:, 0], fkv_new[:, 1], page),
                           logf_s[:, :, None], cache_fkv_t, cache_lf_t, m_suf, l, pp)
        selb, self_sel = _dsa_select(page_table, att_s[0, :, C_IQ:C_IQ + W_IQ].reshape(bd, H_IDX, D_IDX),
                                     misc_s[0, :, M_IW:M_IW + H_IDX][:, :, None],
                                     misc_s[0, :, None, M_IK:M_IK + D_IDX], cache_idx_t, u_strict, l_strict, l, pp)
        dkv_new = dkv_s.reshape(bd, 2, KV_DSA, D_HEAD)[:, :, grp]
        o_ds = _dsa_decode(page_table, *_decode_operands(att_s[0, :, C_DQ:C_DQ + W_DSA].reshape(bd, H_DSA, D_HEAD),
                                                         dkv_new[:, 0], dkv_new[:, 1], page),
                           selb, self_sel, cache_dkv_t, l, pp)
        xs = _merge(o_fs.reshape(1, bd, W_FOX), o_ds.reshape(1, bd, W_DSA), w_o, ms[2], xs, bd)
        outs[4].append(fkv_s.reshape(bd, 1, 2, H_FOX, D_HEAD))
        outs[5].append(logf_s.reshape(bd, 1, H_FOX))
        outs[6].append(dkv_s.reshape(bd, 1, 2, KV_DSA, D_HEAD))
        outs[7].append(misc_s[0, :, M_IK:M_IK + D_IDX].reshape(bd, 1, D_IDX))

        wq_t = jnp.transpose(peer_wq[l]).astype(CD)
        k1p = jnp.pad(peer_k1[l], ((0, 0), (0, 0), (0, D_KEY // 2)))
        k2p = jnp.pad(peer_k2[l], ((0, 0), (0, 0), (D_KEY // 2, 0)))
        kh = jnp.concatenate([k1p, k2p], axis=1)
        eye = jnp.eye(PEER_HEADS, dtype=kh.dtype)
        kbd_t = (kh[:, :, None, :] * eye[:, None, :, None]).reshape(PEER_HEADS * 2 * N_KEYS,
                                                                    PEER_HEADS * D_KEY).astype(CD)
        u_c = peer_u[l].astype(CD)
        v_c = peer_v[l].astype(CD)
        xp = _peer_b(*_peer_a(xp, mp[3], mp[4], g_f, wq_t, kbd_t, tm_w), u_c, v_c, xp, mp[5], tm_p, PEER_IC)
        xs = _peer_b(*_peer_a(xs, ms[3], ms[4], g_f, wq_t, kbd_t, bd), u_c, v_c, xs, ms[5], bd, PEER_IC)

    stk = [jnp.stack(o) for o in outs]
    return (xp, xs.reshape(bd, 1, d), stk[0], stk[1], stk[2], stk[3], stk[4], stk[5], stk[6], stk[7])
```

```python
import functools

import jax
import jax.numpy as jnp
import numpy as np
from jax import lax
from jax.experimental import pallas as pl
from jax.experimental.pallas import tpu as pltpu

D_HEAD = 64
H_FOX = 8
H_DSA = 8
KV_DSA = 2
H_IDX = 8
D_IDX = 64
TOPK_DSA = 256
PEER_HEADS = 8
N_KEYS = 128
D_KEY = 128
PEER_TOPK = 16
EPS = 1e-6

LANES = 128
F32 = jnp.float32
CD = jnp.bfloat16
VMEM_LIMIT = 48 * 1024 * 1024

W_FOX = H_FOX * D_HEAD
W_DSA = H_DSA * D_HEAD
W_DKV = KV_DSA * D_HEAD
W_IQ = H_IDX * D_IDX
C_FQ, C_FK, C_FV, C_DQ, C_IQ = 0, 512, 1024, 1536, 2048
C_DK, C_DV, C_MISC, N_PERM = 2560, 2688, 2816, 2944
M_IK, M_FF, M_IW = 0, 64, 72

NEG_INF = float("-inf")
INT_MIN = -2147483648
KEY_NEG_INF = -2139095041
POS_RADIX = 64
QK_SCALE = D_HEAD ** -0.5

TILE_ROWS = 512
PROJ_ROWS = 256
DSA_TQ = 256
DSA_KC = 512
PEER_IC = 8
PEER_SUB = 32
DEC_PAGES = 8


def _cparams(sem):
    return pltpu.CompilerParams(dimension_semantics=sem, vmem_limit_bytes=VMEM_LIMIT)


def _dot(a, b):
    return jnp.dot(a, b, preferred_element_type=F32)


def _dot_nt(a, b):
    return lax.dot_general(a, b, (((1,), (1,)), ((), ())), preferred_element_type=F32)


def _dot_tn(a, b):
    return lax.dot_general(a, b, (((0,), (0,)), ((), ())), preferred_element_type=F32)


def _dot_split(x, m, n_split, dot=_dot):
    acc = None
    r = x
    for i in range(n_split):
        p = r.astype(CD)
        d = dot(p, m)
        acc = d if acc is None else acc + d
        if i + 1 < n_split:
            r = r - p.astype(F32)
    return acc


def _dot3(a, b, dot):
    a_hi = a.astype(CD)
    a_lo = (a - a_hi.astype(F32)).astype(CD)
    b_hi = b.astype(CD)
    b_lo = (b - b_hi.astype(F32)).astype(CD)
    return dot(a_hi, b_hi) + (dot(a_hi, b_lo) + dot(a_lo, b_hi))


def _split3(x):
    p1 = x.astype(CD)
    r = x - p1.astype(F32)
    p2 = r.astype(CD)
    p3 = (r - p2.astype(F32)).astype(CD)
    return p1, p2, p3


def _rms_mod(x, g, shift, scale):
    ms = jnp.mean(x * x, axis=-1, keepdims=True)
    y = x * lax.rsqrt(ms + EPS) * g
    return y * (1.0 + scale) + shift


def _softmax_step(s, v, m_ref, l_ref, acc_ref):
    m_prev = m_ref[...]
    m_new = jnp.maximum(m_prev, jnp.max(s, axis=1, keepdims=True))
    m_safe = jnp.where(m_new == NEG_INF, 0.0, m_new)
    alpha = jnp.exp(m_prev - m_safe)
    p = jnp.exp(s - m_safe)
    l_ref[...] = alpha * l_ref[...] + jnp.sum(p, axis=1, keepdims=True)
    acc_ref[...] = alpha * acc_ref[...] + _dot(p.astype(CD), v)
    m_ref[...] = m_new


def _ada_kernel(c_ref, w_ref, b_ref, o_ref):
    c = c_ref[...]
    a = (c * jax.nn.sigmoid(c)).astype(CD)
    o_ref[...] = _dot(a, w_ref[...].astype(CD)) + b_ref[...]


def _ada(c_all, w_ada, b_ada):
    depth, d, n6 = w_ada.shape
    r = c_all.shape[0]
    tn = n6 // 4
    return pl.pallas_call(
        _ada_kernel,
        grid=(depth, n6 // tn),
        in_specs=[
            pl.BlockSpec((r, d), lambda l, j: (0, 0)),
            pl.BlockSpec((None, d, tn), lambda l, j: (l, 0, j)),
            pl.BlockSpec((None, 1, tn), lambda l, j: (l, 0, j)),
        ],
        out_specs=pl.BlockSpec((None, r, tn), lambda l, j: (l, 0, j)),
        out_shape=jax.ShapeDtypeStruct((depth, r, n6), F32),
        compiler_params=_cparams(("arbitrary", "arbitrary")),
        name="ada",
    )(c_all, w_ada, b_ada.reshape(depth, 1, n6))


def _head_norm(z, gain, seg):
    sq = z * z
    ss = _dot_split(sq, seg, 2)
    return z * lax.rsqrt(ss * (1.0 / D_HEAD) + EPS) * gain


def _proj_kernel(x_ref, sh_ref, sc_ref, g_ref, w_ref, bias_ref, gains_ref, seg_ref,
                 fkv_ref, dkv_ref, misc_ref, att_ref, q32_ref):
    h = _rms_mod(x_ref[...], g_ref[...], sh_ref[...], sc_ref[...]).astype(CD)
    seg = seg_ref[...]

    def seg_dot(c0, width):
        return _dot(h, w_ref[:, c0:c0 + width])

    fq = _head_norm(seg_dot(C_FQ, W_FOX), gains_ref[0:1, :], seg)
    att_ref[:, C_FQ:C_FQ + W_FOX] = fq.astype(CD)
    q32_ref[:, 0:W_FOX] = fq
    fk = _head_norm(seg_dot(C_FK, W_FOX), gains_ref[1:2, :], seg)
    fkv_ref[:, 0:W_FOX] = fk
    att_ref[:, C_FK:C_FK + W_FOX] = fk.astype(CD)
    fv = seg_dot(C_FV, W_FOX)
    fkv_ref[:, W_FOX:2 * W_FOX] = fv
    att_ref[:, C_FV:C_FV + W_FOX] = fv.astype(CD)
    dq = _head_norm(seg_dot(C_DQ, W_DSA), gains_ref[2:3, :], seg)
    att_ref[:, C_DQ:C_DQ + W_DSA] = dq.astype(CD)
    q32_ref[:, W_FOX:W_FOX + W_DSA] = dq
    iq = seg_dot(C_IQ, W_IQ)
    att_ref[:, C_IQ:C_IQ + W_IQ] = iq.astype(CD)
    q32_ref[:, W_FOX + W_DSA:W_FOX + W_DSA + W_IQ] = iq
    dk = _head_norm(seg_dot(C_DK, W_DKV), gains_ref[3:4, 0:W_DKV], seg[0:W_DKV, 0:W_DKV])
    dkv_ref[:, 0:W_DKV] = dk
    att_ref[:, C_DK:C_DK + W_DKV] = dk.astype(CD)
    dv = seg_dot(C_DV, W_DKV)
    dkv_ref[:, W_DKV:2 * W_DKV] = dv
    att_ref[:, C_DV:C_DV + W_DKV] = dv.astype(CD)
    zm = seg_dot(C_MISC, LANES)
    zb = zm + bias_ref[...]
    logsig = jnp.minimum(zb, 0.0) - jnp.log1p(jnp.exp(-jnp.abs(zb)))
    lane = lax.broadcasted_iota(jnp.int32, zm.shape, 1)
    misc = jnp.where((lane >= M_FF) & (lane < M_FF + H_FOX), logsig, zm)
    misc_ref[...] = misc
    att_ref[:, C_MISC:C_MISC + LANES] = misc.astype(CD)


def _mod_spec(mod, tm, d):
    if mod.shape[1] == 1:
        return pl.BlockSpec((None, 1, d), lambda b, i: (b, 0, 0))
    return pl.BlockSpec((None, tm, d), lambda b, i: (b, i, 0))


def _proj(x, shift, scale, g, w_perm, bias_row, gains, seg, tm):
    bsz, t, d = x.shape
    row = lambda width: pl.BlockSpec((None, tm, width), lambda b, i: (b, i, 0))
    const = lambda shape: pl.BlockSpec(shape, lambda b, i: (0, 0))
    return pl.pallas_call(
        _proj_kernel,
        grid=(bsz, t // tm),
        in_specs=[row(d), _mod_spec(shift, tm, d), _mod_spec(scale, tm, d), const((1, d)),
                  const((d, N_PERM)), const((1, LANES)), const((4, W_FOX)), const((W_FOX, W_FOX))],
        out_specs=[row(2 * W_FOX), row(2 * W_DKV), row(LANES), row(N_PERM), row(W_FOX + W_DSA + W_IQ)],
        out_shape=[jax.ShapeDtypeStruct((bsz, t, 2 * W_FOX), F32),
                   jax.ShapeDtypeStruct((bsz, t, 2 * W_DKV), F32),
                   jax.ShapeDtypeStruct((bsz, t, LANES), F32),
                   jax.ShapeDtypeStruct((bsz, t, N_PERM), CD),
                   jax.ShapeDtypeStruct((bsz, t, W_FOX + W_DSA + W_IQ), F32)],
        compiler_params=_cparams(("arbitrary", "arbitrary")),
        name="proj",
    )(x, shift, scale, g, w_perm, bias_row, gains, seg)


def _cumsum_kernel(x_ref, u_ref, p1_ref, p2_ref, p3_ref):
    t = x_ref.shape[-1]
    u = u_ref[...]
    carry = jnp.zeros((x_ref.shape[0], 1), F32)
    for j in range(t // LANES):
        cols = slice(j * LANES, (j + 1) * LANES)
        c = _dot_split(x_ref[:, cols], u, 3) + carry
        p1, p2, p3 = _split3(c)
        p1_ref[:, cols] = p1
        p2_ref[:, cols] = p2
        p3_ref[:, cols] = p3
        carry = c[:, LANES - 1:LANES]


def _cumsum(lf_t, u_incl):
    bsz, h, t = lf_t.shape
    spec = pl.BlockSpec((None, h, t), lambda b: (b, 0, 0))
    piece = jax.ShapeDtypeStruct((bsz, h, t), CD)
    return pl.pallas_call(
        _cumsum_kernel,
        grid=(bsz,),
        in_specs=[spec, pl.BlockSpec((LANES, LANES), lambda b: (0, 0))],
        out_specs=[spec, spec, spec],
        out_shape=[piece, piece, piece],
        compiler_params=_cparams(("arbitrary",)),
        name="cumsum",
    )(lf_t, u_incl)


def _fox_kernel(q_ref, k_ref, v_ref, o_ref, m_scr, l_scr, acc_scr, *, tq):
    qi = pl.program_id(2)
    q = q_ref[...]
    row = lax.broadcasted_iota(jnp.int32, (tq, tq), 0)
    col = lax.broadcasted_iota(jnp.int32, (tq, tq), 1)

    def logits(c):
        return _dot_nt(q, k_ref[pl.ds(pl.multiple_of(c * tq, tq), tq), :])

    def diagonal():
        return jnp.where(col <= row, logits(qi), NEG_INF)

    m_scr[...] = _lane_fold(diagonal(), jnp.maximum)

    def max_body(c, carry):
        m_scr[...] = jnp.maximum(m_scr[...], _lane_fold(logits(c), jnp.maximum))
        return carry

    lax.fori_loop(0, qi, max_body, 0)
    m_scr[...] = jnp.broadcast_to(jnp.max(m_scr[...], axis=1, keepdims=True), m_scr.shape)
    l_scr[...] = jnp.zeros(l_scr.shape, F32)
    acc_scr[...] = jnp.zeros(acc_scr.shape, F32)

    def accumulate(s, c):
        m = m_scr[...]
        p = [jnp.exp(s[:, j * LANES:(j + 1) * LANES] - m) for j in range(tq // LANES)]
        l_scr[...] += functools.reduce(jnp.add, p)
        acc_scr[...] += _dot(jnp.concatenate(p, axis=1).astype(CD),
                             v_ref[pl.ds(pl.multiple_of(c * tq, tq), tq), :])

    def att_body(c, carry):
        accumulate(logits(c), c)
        return carry

    lax.fori_loop(0, qi, att_body, 0)
    accumulate(diagonal(), qi)
    o_ref[...] = acc_scr[...] / jnp.sum(l_scr[...], axis=1, keepdims=True)


def _fox_prompt(q_aug, k_aug, v, tq):
    bsz, h, t, dk = q_aug.shape
    dh = v.shape[-1]
    full = lambda w: pl.BlockSpec((None, None, t, w), lambda b, hh, i: (b, hh, 0, 0))
    return pl.pallas_call(
        functools.partial(_fox_kernel, tq=tq),
        grid=(bsz, h, t // tq),
        in_specs=[pl.BlockSpec((None, None, tq, dk), lambda b, hh, i: (b, hh, i, 0)), full(dk), full(dh)],
        out_specs=pl.BlockSpec((None, None, tq, dh), lambda b, hh, i: (b, hh, i, 0)),
        out_shape=jax.ShapeDtypeStruct((bsz, h, t, dh), F32),
        scratch_shapes=[pltpu.VMEM((tq, LANES), F32), pltpu.VMEM((tq, LANES), F32), pltpu.VMEM((tq, dh), F32)],
        compiler_params=_cparams(("arbitrary",) * 3),
        name="fox_prompt",
    )(q_aug, k_aug, v)


def _sort_key(x):
    b = pltpu.bitcast(x, jnp.int32)
    return b ^ (lax.shift_right_arithmetic(b, 31) & jnp.int32(0x7FFFFFFF))


def _kth_largest_key(count_ge, shape, k):
    def body(i, res):
        cand = res + lax.shift_left(jnp.int32(1), jnp.int32(31) - i)
        return jnp.where(count_ge(cand) >= k, cand, res)

    return lax.fori_loop(0, 32, body, jnp.full(shape, INT_MIN, jnp.int32))


def _count(mask):
    return jnp.sum(jnp.where(mask, 1.0, 0.0), axis=1, keepdims=True)


def _lane_fold(x, op):
    return functools.reduce(op, [x[:, j * LANES:(j + 1) * LANES] for j in range(x.shape[1] // LANES)])


def _dsa_kernel(iq_ref, iw_ref, ik_ref, dq_ref, dk_ref, dv_ref, us_ref, o_ref,
                key_scr, selb_scr, m_scr, l_scr, acc_scr, *, tq, kc, k_sel):
    qi = pl.program_id(1)
    n_c = (qi * tq + tq + kc - 1) // kc
    w = iw_ref[...] * (H_IDX ** -0.5) * (D_IDX ** -0.5)
    qpos = qi * tq + lax.broadcasted_iota(jnp.int32, (tq, kc), 0)
    kk = float(k_sel)

    def score_body(c, carry):
        k0 = pl.multiple_of(c * kc, kc)
        ikc = ik_ref[pl.ds(k0, kc), :]
        score = jnp.zeros((tq, kc), F32)
        for h in range(H_IDX):
            score = score + w[:, h:h + 1] * jnp.maximum(_dot_nt(iq_ref[h], ikc), 0.0)
        kpos = k0 + lax.broadcasted_iota(jnp.int32, (tq, kc), 1)
        key_scr[c] = _sort_key(jnp.where(kpos <= qpos, score, NEG_INF))
        return carry

    lax.fori_loop(0, n_c, score_body, 0)

    def chunk_count(pred):
        part = lax.fori_loop(
            0, n_c, lambda c, a: a + _lane_fold(jnp.where(pred(key_scr[c]), 1.0, 0.0), jnp.add),
            jnp.zeros((tq, LANES), F32))
        return jnp.sum(part, axis=1, keepdims=True)

    thr = _kth_largest_key(lambda cand: chunk_count(lambda key: key >= cand), (tq, 1), kk)
    is_gt = lambda key: (key > thr) & (key != KEY_NEG_INF)
    is_eq = lambda key: (key == thr) & (key != KEY_NEG_INF)
    need = kk - chunk_count(is_gt)
    n_eq = chunk_count(is_eq)

    def selb_body(c, carry):
        key = key_scr[c]
        selb_scr[c] = jnp.where(is_gt(key) | is_eq(key), 0.0, NEG_INF)
        return carry

    lax.fori_loop(0, n_c, selb_body, 0)

    @pl.when(jnp.max(n_eq - need) > 0.0)
    def _():
        us = us_ref[...]

        def tie_body(c, before):
            key = key_scr[c]
            eq = is_eq(key)
            eqf = jnp.where(eq, 1.0, 0.0).astype(CD)
            ranks = []
            for j in range(kc // LANES):
                blk = eqf[:, j * LANES:(j + 1) * LANES]
                ranks.append(_dot(blk, us) + before)
                before = before + jnp.sum(blk.astype(F32), axis=1, keepdims=True)
            rank = jnp.concatenate(ranks, axis=1)
            selb_scr[c] = jnp.where(is_gt(key) | (eq & (rank < need)), 0.0, NEG_INF)
            return before

        lax.fori_loop(0, n_c, tie_body, jnp.zeros((tq, 1), F32))

    m_scr[...] = jnp.full(m_scr.shape, NEG_INF, F32)
    l_scr[...] = jnp.zeros(l_scr.shape, F32)
    acc_scr[...] = jnp.zeros(acc_scr.shape, F32)
    rep = H_DSA // KV_DSA

    def logits(c, h):
        k0 = pl.multiple_of(c * kc, kc)
        return _dot_nt(dq_ref[h], dk_ref[h // rep, pl.ds(k0, kc), :]) + selb_scr[c]

    def max_body(c, carry):
        for h in range(H_DSA):
            m_scr[h] = jnp.maximum(m_scr[h], _lane_fold(logits(c, h), jnp.maximum))
        return carry

    lax.fori_loop(0, n_c, max_body, 0)
    for h in range(H_DSA):
        m_scr[h] = jnp.broadcast_to(jnp.max(m_scr[h], axis=1, keepdims=True), (tq, LANES))

    def att_body(c, carry):
        k0 = pl.multiple_of(c * kc, kc)
        for h in range(H_DSA):
            s = logits(c, h)
            m = m_scr[h]
            p = [jnp.exp(s[:, j * LANES:(j + 1) * LANES] - m) for j in range(kc // LANES)]
            l_scr[h] = l_scr[h] + functools.reduce(jnp.add, p)
            acc_scr[h] = acc_scr[h] + _dot(jnp.concatenate(p, axis=1).astype(CD),
                                           dv_ref[h // rep, pl.ds(k0, kc), :])
        return carry

    lax.fori_loop(0, n_c, att_body, 0)
    for h in range(H_DSA):
        o_ref[h] = acc_scr[h] / jnp.sum(l_scr[h], axis=1, keepdims=True)


def _dsa_prompt(iq, iw, ik, dq_aug, dk_aug, dv, u_strict, tq, kc):
    bsz, hi, t, di = iq.shape
    dka = dq_aug.shape[-1]
    k_sel = min(TOPK_DSA, t // 4)
    qh = lambda nh, w: pl.BlockSpec((None, nh, tq, w), lambda b, i: (b, 0, i, 0))
    full = lambda nh, w: pl.BlockSpec((None, nh, t, w), lambda b, i: (b, 0, 0, 0))
    return pl.pallas_call(
        functools.partial(_dsa_kernel, tq=tq, kc=kc, k_sel=k_sel),
        grid=(bsz, t // tq),
        in_specs=[qh(hi, di),
                  pl.BlockSpec((None, tq, H_IDX), lambda b, i: (b, i, 0)),
                  pl.BlockSpec((None, t, di), lambda b, i: (b, 0, 0)),
                  qh(H_DSA, dka), full(KV_DSA, dka), full(KV_DSA, D_HEAD),
                  pl.BlockSpec((LANES, LANES), lambda b, i: (0, 0))],
        out_specs=qh(H_DSA, D_HEAD),
        out_shape=jax.ShapeDtypeStruct((bsz, H_DSA, t, D_HEAD), F32),
        scratch_shapes=[pltpu.VMEM((t // kc, tq, kc), jnp.int32), pltpu.VMEM((t // kc, tq, kc), F32),
                        pltpu.VMEM((H_DSA, tq, LANES), F32), pltpu.VMEM((H_DSA, tq, LANES), F32),
                        pltpu.VMEM((H_DSA, tq, D_HEAD), F32)],
        compiler_params=_cparams(("arbitrary", "arbitrary")),
        name="dsa_prompt",
    )(iq, iw, ik, dq_aug, dk_aug, dv, u_strict)


def _merge_kernel(of_ref, od_ref, w_ref, gate_ref, x_ref, o_ref):
    half = of_ref.shape[-1]
    y = _dot(of_ref[...].astype(CD), w_ref[0:half, :]) + _dot(od_ref[...].astype(CD), w_ref[half:, :])
    o_ref[...] = x_ref[...] + gate_ref[...] * y


def _merge(o_f, o_d, w_out, gate, x, tm):
    bsz, t, d = x.shape
    half = o_f.shape[-1]
    row = lambda width: pl.BlockSpec((None, tm, width), lambda b, i: (b, i, 0))
    return pl.pallas_call(
        _merge_kernel,
        grid=(bsz, t // tm),
        in_specs=[row(half), row(half), pl.BlockSpec((2 * half, d), lambda b, i: (0, 0)),
                  _mod_spec(gate, tm, d), row(d)],
        out_specs=row(d),
        out_shape=jax.ShapeDtypeStruct((bsz, t, d), F32),
        compiler_params=_cparams(("arbitrary", "arbitrary")),
        name="merge",
    )(o_f, o_d, w_out, gate, x)


def _top_rows(x, n):
    vals = []
    for _ in range(n):
        m = jnp.max(x, axis=0, keepdims=True)
        vals.append(m)
        x = jnp.where(x == m, NEG_INF, x)
    return vals


_PEER_PAIRS = [(i, j) for i in range(PEER_TOPK) for j in range(PEER_TOPK) if (i + 1) * (j + 1) <= PEER_TOPK]
_PEER_CAND_ROWS = -(-len(_PEER_PAIRS) // 8) * 8


def _peer_a_kernel(x_ref, sh_ref, sc_ref, g_ref, wqt_ref, kbdt_ref, hq_ref, cut_ref, e1_ref, s2_ref, e2_ref,
                   cand_scr, candw_scr, *, precise):
    hq32 = _rms_mod(x_ref[...], g_ref[...], sh_ref[...], sc_ref[...])
    hq = hq32.astype(CD)
    hq_ref[...] = hq
    if precise:
        qt = _dot3(wqt_ref[...], hq32, _dot_nt)
        st = jnp.concatenate([_dot3(kbdt_ref[h], qt[h * D_KEY:(h + 1) * D_KEY], _dot)
                              for h in range(PEER_HEADS)], axis=0)
    else:
        qt = _dot_nt(wqt_ref[...], hq)
        st = _dot(kbdt_ref[...], qt.astype(CD))
    cand_scr[...] = jnp.full(cand_scr.shape, NEG_INF, F32)
    candw_scr[...] = jnp.zeros(candw_scr.shape, F32)
    for h in range(PEER_HEADS):
        r0 = h * 2 * N_KEYS
        s1 = st[r0:r0 + N_KEYS]
        s2 = st[r0 + N_KEYS:r0 + 2 * N_KEYS]
        v1 = _top_rows(s1, PEER_TOPK)
        v2 = _top_rows(s2, PEER_TOPK)
        w1 = [jnp.exp(a - v1[0]) for a in v1]
        w2 = [jnp.exp(a - v2[0]) for a in v2]
        sums = {}
        for r, (i, j) in enumerate(_PEER_PAIRS):
            sums[i, j] = v1[i] + v2[j]
            cand_scr[r:r + 1, :] = sums[i, j]
            candw_scr[r:r + 1, :] = w1[i] * w2[j]
        cand = cand_scr[...]
        tau = _top_rows(cand, PEER_TOPK)[-1]
        z = jnp.sum(jnp.where(cand >= tau, candw_scr[...], 0.0), axis=0, keepdims=True)
        cut = jnp.full(s1.shape, jnp.inf, F32)
        for i in range(PEER_TOPK):
            cut_i = jnp.full(tau.shape, jnp.inf, F32)
            for j in range(PEER_TOPK // (i + 1)):
                cut_i = jnp.where(sums[i, j] >= tau, v2[j], cut_i)
            cut = jnp.where(s1 == v1[i], cut_i, cut)
        h0 = h * N_KEYS
        lbw = cut_ref.shape[-1]
        e1 = jnp.exp(s1 - v1[0])
        e2 = jnp.exp(s2 - v2[0]) / z
        for j in range(cut_ref.shape[0]):
            cols = slice(j * lbw, (j + 1) * lbw)
            cut_ref[j, h0:h0 + N_KEYS, :] = cut[:, cols]
            e1_ref[j, h0:h0 + N_KEYS, :] = e1[:, cols]
            s2_ref[j, h0:h0 + N_KEYS, :] = s2[:, cols]
            e2_ref[j, h0:h0 + N_KEYS, :] = e2[:, cols]


def _peer_a(x, shift, scale, g, wq_t, kbd_t, tm):
    bsz, t, d = x.shape
    ns = PEER_HEADS * N_KEYS
    lbw = min(LANES, tm)
    row = pl.BlockSpec((None, tm, d), lambda b, i: (b, i, 0))
    col = pl.BlockSpec((None, tm // lbw, ns, lbw), lambda b, i: (b, i, 0, 0))
    const = lambda shape: pl.BlockSpec(shape, lambda b, i: (0,) * len(shape))
    cols = jax.ShapeDtypeStruct((bsz, t // lbw, ns, lbw), F32)
    return pl.pallas_call(
        functools.partial(_peer_a_kernel, precise=kbd_t.ndim == 3),
        grid=(bsz, t // tm),
        in_specs=[row, _mod_spec(shift, tm, d), _mod_spec(scale, tm, d), const((1, d)),
                  const(wq_t.shape), const(kbd_t.shape)],
        out_specs=[row, col, col, col, col],
        out_shape=[jax.ShapeDtypeStruct((bsz, t, d), CD), cols, cols, cols, cols],
        scratch_shapes=[pltpu.VMEM((_PEER_CAND_ROWS, tm), F32), pltpu.VMEM((_PEER_CAND_ROWS, tm), F32)],
        compiler_params=_cparams(("arbitrary", "arbitrary")),
        name="peer_a",
    )(x, shift, scale, g, wq_t, kbd_t)


def _peer_b_kernel(hq_ref, cut_ref, e1_ref, s2_ref, e2_ref, u_ref, v_ref, x_ref, gate_ref, o_ref,
                   acc_scr, coef_scr, gel_scr, *, ic, sub):
    c = pl.program_id(2)
    nlb, _, lbw = cut_ref.shape

    @pl.when(c == 0)
    def _():
        acc_scr[...] = jnp.zeros(acc_scr.shape, F32)

    act = _dot_nt(u_ref[...], hq_ref[...])
    for j in range(nlb):
        a = act[:, j * lbw:(j + 1) * lbw]
        gel_scr[j] = 0.5 * a * (1.0 + lax.erf(a * (2.0 ** -0.5)))

    n_r = N_KEYS // sub

    def tile_body(t, carry):
        j = t // n_r
        r = pl.multiple_of((t % n_r) * sub, sub)
        gates = [jnp.zeros((sub, lbw), F32) for _ in range(ic)]
        for h in range(PEER_HEADS):
            s2 = s2_ref[j, pl.ds(h * N_KEYS + r, sub), :]
            e2 = e2_ref[j, pl.ds(h * N_KEYS + r, sub), :]
            for ii in range(ic):
                cut = cut_ref[j, pl.ds(h * N_KEYS + c * ic + ii, 1), :]
                e1 = e1_ref[j, pl.ds(h * N_KEYS + c * ic + ii, 1), :]
                gates[ii] = gates[ii] + jnp.where(s2 >= cut, e1 * e2, 0.0)
        for ii in range(ic):
            rows = pl.ds(ii * N_KEYS + r, sub)
            coef_scr[j, rows, :] = (gates[ii] * gel_scr[j, rows, :]).astype(CD)
        return carry

    lax.fori_loop(0, nlb * n_r, tile_body, 0)
    coef = jnp.concatenate([coef_scr[j] for j in range(nlb)], axis=1)
    acc_scr[...] += _dot_tn(coef, v_ref[...])

    @pl.when(c == pl.num_programs(2) - 1)
    def _():
        o_ref[...] = x_ref[...] + gate_ref[...] * acc_scr[...]


def _peer_b(hq, cut, e1, s2, e2, u_c, v_c, x, gate, tm, ic):
    bsz, t, d = x.shape
    _, _, ns, lbw = cut.shape
    nlb = tm // lbw
    ec = ic * N_KEYS
    row = pl.BlockSpec((None, tm, d), lambda b, i, c: (b, i, 0))
    col = pl.BlockSpec((None, nlb, ns, lbw), lambda b, i, c: (b, i, 0, 0))
    wspec = pl.BlockSpec((ec, d), lambda b, i, c: (c, 0))
    if gate.shape[1] == 1:
        gspec = pl.BlockSpec((None, 1, d), lambda b, i, c: (b, 0, 0))
    else:
        gspec = row
    return pl.pallas_call(
        functools.partial(_peer_b_kernel, ic=ic, sub=PEER_SUB),
        grid=(bsz, t // tm, N_KEYS // ic),
        in_specs=[row, col, col, col, col, wspec, wspec, row, gspec],
        out_specs=row,
        out_shape=jax.ShapeDtypeStruct((bsz, t, d), F32),
        scratch_shapes=[pltpu.VMEM((tm, d), F32), pltpu.VMEM((nlb, ec, lbw), CD),
                        pltpu.VMEM((nlb, ec, lbw), F32)],
        compiler_params=_cparams(("arbitrary",) * 3),
        name="peer_b",
    )(hq, cut, e1, s2, e2, u_c, v_c, x, gate)


def _qk_rows(qb_ref, kt_of_head, n_heads):
    return jnp.concatenate([jnp.sum(qb_ref[h] * kt_of_head(h), axis=0, keepdims=True) for h in range(n_heads)],
                           axis=0)


def _pv_update(acc_scr, alpha, probs, vt_of):
    for h in range(acc_scr.shape[0]):
        a = alpha[h:h + 1, :] * acc_scr[h]
        for p, prob in enumerate(probs):
            a = a + prob[h:h + 1, :] * vt_of(p, h)
        acc_scr[h] = a


def _fox_dec_kernel(pt_ref, q_ref, qb_ref, kn_ref, vcol_ref, lfn_ref, msuf_ref, *rest, pp):
    kv_refs = rest[:pp]
    lf_refs = rest[pp:2 * pp]
    o_ref, m_scr, l_scr, acc_scr, carry_scr = rest[2 * pp:]
    j = pl.program_id(1)

    @pl.when(j == 0)
    def _():
        m_scr[...] = jnp.sum(q_ref[...] * kn_ref[...], axis=1, keepdims=True)
        l_scr[...] = jnp.ones(l_scr.shape, F32)
        acc_scr[...] = vcol_ref[...]
        carry_scr[...] = lfn_ref[...]

    carry = carry_scr[...]
    logits = []
    for p in range(pp):
        lft = lf_refs[p][...]
        decay = _dot_split(lft, msuf_ref[...], 3) + carry
        logits.append(_qk_rows(qb_ref, lambda h: kv_refs[p][0, h], H_FOX) + decay)
        carry = carry + jnp.sum(lft, axis=1, keepdims=True)
    carry_scr[...] = carry
    top = functools.reduce(jnp.maximum, logits)
    m_prev = m_scr[...]
    m_new = jnp.maximum(m_prev, jnp.max(top, axis=1, keepdims=True))
    alpha = jnp.exp(m_prev - m_new)
    probs = [jnp.exp(s - m_new) for s in logits]
    _pv_update(acc_scr, alpha, probs, lambda p, h: kv_refs[p][1, h])
    l_scr[...] = alpha * l_scr[...] + jnp.sum(functools.reduce(jnp.add, probs), axis=1, keepdims=True)
    m_scr[...] = m_new

    @pl.when(j == pl.num_programs(1) - 1)
    def _():
        for h in range(H_FOX):
            o_ref[h] = jnp.sum(acc_scr[h], axis=1, keepdims=True) / l_scr[h:h + 1, :]


def _fox_decode(page_table, q, qb, k_new, v_col, lf_new, cache_t, cache_lft, m_suf, layer, pp):
    bd, n_pages = page_table.shape
    page = cache_t.shape[-1]
    per_b = lambda shape: pl.BlockSpec((None,) + shape, lambda b, j, pt: (b,) + (0,) * len(shape))

    def page_spec(shape, p):
        return pl.BlockSpec((None, None) + shape,
                            lambda b, j, pt: (layer, pt[b, n_pages - 1 - (j * pp + p)]) + (0,) * len(shape))

    grid_spec = pltpu.PrefetchScalarGridSpec(
        num_scalar_prefetch=1,
        grid=(bd, n_pages // pp),
        in_specs=[per_b((H_FOX, D_HEAD)), per_b((H_FOX, D_HEAD, page)), per_b((H_FOX, D_HEAD)),
                  per_b((H_FOX, D_HEAD, page)), per_b((H_FOX, 1)),
                  pl.BlockSpec((page, page), lambda b, j, pt: (0, 0))]
                 + [page_spec((2, H_FOX, D_HEAD, page), p) for p in range(pp)]
                 + [page_spec((H_FOX, page), p) for p in range(pp)],
        out_specs=per_b((H_FOX, D_HEAD, 1)),
        scratch_shapes=[pltpu.VMEM((H_FOX, 1), F32), pltpu.VMEM((H_FOX, 1), F32),
                        pltpu.VMEM((H_FOX, D_HEAD, page), F32), pltpu.VMEM((H_FOX, 1), F32)],
    )
    return pl.pallas_call(
        functools.partial(_fox_dec_kernel, pp=pp),
        grid_spec=grid_spec,
        out_shape=jax.ShapeDtypeStruct((bd, H_FOX, D_HEAD, 1), F32),
        compiler_params=_cparams(("arbitrary", "arbitrary")),
        name="fox_decode",
    )(page_table, q, qb, k_new, v_col, lf_new, m_suf, *([cache_t] * pp), *([cache_lft] * pp))


def _dsa_sel_kernel(pt_ref, iq_ref, iw_ref, ikn_ref, us_ref, ls_ref, *rest, pp, n_pages, k_sel):
    ik_refs = rest[:pp]
    selb_ref, self_ref, sc_scr = rest[pp:]
    j = pl.program_id(1)
    page = ik_refs[0].shape[1]
    w = iw_ref[...] * (H_IDX ** -0.5) * (D_IDX ** -0.5)
    iq = iq_ref[...]
    for p in range(pp):
        r = jnp.maximum(_dot3(iq, ik_refs[p][...], _dot), 0.0)
        sc_scr[pl.ds(j * pp + p, 1), :] = jnp.sum(w * r, axis=0, keepdims=True)

    @pl.when(j == pl.num_programs(1) - 1)
    def _():
        ikn = ikn_ref[...]
        r_self = jnp.maximum(jnp.sum(iq * ikn, axis=1, keepdims=True), 0.0)
        s_self = jnp.sum(w * r_self, axis=0, keepdims=True)
        key = _sort_key(sc_scr[...])
        key_self = _sort_key(s_self)
        total = lambda m: jnp.sum(_count(m), axis=0, keepdims=True)

        def count_ge(cand):
            return total(key >= cand) + jnp.where(key_self >= cand, 1.0, 0.0)

        thr = _kth_largest_key(count_ge, (1, 1), float(k_sel))
        gt = key > thr
        eq = key == thr
        eqf = jnp.where(eq, 1.0, 0.0)
        need = float(k_sel) - (total(gt) + jnp.where(key_self > thr, 1.0, 0.0))
        row_eq = jnp.sum(eqf, axis=1, keepdims=True)
        before = _dot(ls_ref[...], jnp.broadcast_to(row_eq, (n_pages, page)).astype(CD))
        rank = _dot(eqf.astype(CD), us_ref[...]) + before
        selb_ref[...] = jnp.where(gt | (eq & (rank < need)), 0.0, NEG_INF)
        n_eq_past = jnp.sum(row_eq, axis=0, keepdims=True)
        self_sel = (key_self > thr) | ((key_self == thr) & (n_eq_past < need))
        self_ref[...] = jnp.broadcast_to(jnp.where(self_sel, 1.0, 0.0), self_ref.shape)


def _dsa_select(page_table, iq, iw_col, ik_new, cache_idx, u_strict, l_strict, layer, pp):
    bd, n_pages = page_table.shape
    page = cache_idx.shape[-1]
    k_sel = min(TOPK_DSA, (n_pages * page + 1) // 4)
    per_b = lambda shape: pl.BlockSpec((None,) + shape, lambda b, j, pt: (b, 0, 0))
    const = lambda shape: pl.BlockSpec(shape, lambda b, j, pt: (0, 0))
    grid_spec = pltpu.PrefetchScalarGridSpec(
        num_scalar_prefetch=1,
        grid=(bd, n_pages // pp),
        in_specs=[per_b((H_IDX, D_IDX)), per_b((H_IDX, 1)), per_b((1, D_IDX)),
                  const((page, page)), const((n_pages, n_pages))]
                 + [pl.BlockSpec((None, None, D_IDX, page),
                                 lambda b, j, pt, p=p: (layer, pt[b, j * pp + p], 0, 0)) for p in range(pp)],
        out_specs=[per_b((n_pages, page)), per_b((H_DSA, LANES))],
        scratch_shapes=[pltpu.VMEM((n_pages, page), F32)],
    )
    return pl.pallas_call(
        functools.partial(_dsa_sel_kernel, pp=pp, n_pages=n_pages, k_sel=k_sel),
        grid_spec=grid_spec,
        out_shape=[jax.ShapeDtypeStruct((bd, n_pages, page), F32),
                   jax.ShapeDtypeStruct((bd, H_DSA, LANES), F32)],
        compiler_params=_cparams(("arbitrary", "arbitrary")),
        name="dsa_select",
    )(page_table, iq, iw_col, ik_new, u_strict, l_strict, *([cache_idx] * pp))


def _dsa_dec_kernel(pt_ref, q_ref, qb_ref, kn_ref, vcol_ref, selb_ref, self_ref, *rest, pp, n_pages):
    kv_refs = rest[:pp]
    o_ref, m_scr, l_scr, acc_scr = rest[pp:]
    j = pl.program_id(1)
    page = kv_refs[0].shape[-1]
    rep = H_DSA // KV_DSA

    @pl.when(j == 0)
    def _():
        self_sel = self_ref[:, 0:1] > 0.0
        s_new = jnp.sum(q_ref[...] * kn_ref[...], axis=1, keepdims=True)
        m_scr[...] = jnp.where(self_sel, s_new, NEG_INF)
        l_scr[...] = jnp.where(self_sel, 1.0, 0.0)
        for h in range(H_DSA):
            acc_scr[h] = jnp.where(self_ref[h:h + 1, 0:1] > 0.0, vcol_ref[h], 0.0)

    hh = lax.broadcasted_iota(jnp.int32, (H_DSA, page), 0)
    slope = jnp.exp2(-8.0 * (hh + 1).astype(F32) / H_DSA)
    lane = lax.broadcasted_iota(jnp.int32, (H_DSA, page), 1)
    logits = []
    for p in range(pp):
        pg = j * pp + p
        dist = (n_pages * page - (pg * page + lane)).astype(F32)
        s = _qk_rows(qb_ref, lambda h: kv_refs[p][0, h // rep], H_DSA)
        logits.append(s - slope * dist + selb_ref[pl.ds(pg, 1), :])
    top = functools.reduce(jnp.maximum, logits)
    m_prev = m_scr[...]
    m_new = jnp.maximum(m_prev, jnp.max(top, axis=1, keepdims=True))
    m_safe = jnp.where(m_new == NEG_INF, 0.0, m_new)
    alpha = jnp.exp(m_prev - m_safe)
    probs = [jnp.exp(s - m_safe) for s in logits]
    _pv_update(acc_scr, alpha, probs, lambda p, h: kv_refs[p][1, h // rep])
    l_scr[...] = alpha * l_scr[...] + jnp.sum(functools.reduce(jnp.add, probs), axis=1, keepdims=True)
    m_scr[...] = m_new

    @pl.when(j == pl.num_programs(1) - 1)
    def _():
        for h in range(H_DSA):
            o_ref[h] = jnp.sum(acc_scr[h], axis=1, keepdims=True) / l_scr[h:h + 1, :]


def _dsa_decode(page_table, q, qb, k_new, v_col, selb, self_sel, cache_t, layer, pp):
    bd, n_pages = page_table.shape
    page = cache_t.shape[-1]
    per_b = lambda shape: pl.BlockSpec((None,) + shape, lambda b, j, pt: (b,) + (0,) * len(shape))
    grid_spec = pltpu.PrefetchScalarGridSpec(
        num_scalar_prefetch=1,
        grid=(bd, n_pages // pp),
        in_specs=[per_b((H_DSA, D_HEAD)), per_b((H_DSA, D_HEAD, page)), per_b((H_DSA, D_HEAD)),
                  per_b((H_DSA, D_HEAD, page)), per_b((n_pages, page)), per_b((H_DSA, LANES))]
                 + [pl.BlockSpec((None, None, 2, KV_DSA, D_HEAD, page),
                                 lambda b, j, pt, p=p: (layer, pt[b, j * pp + p], 0, 0, 0, 0)) for p in range(pp)],
        out_specs=per_b((H_DSA, D_HEAD, 1)),
        scratch_shapes=[pltpu.VMEM((H_DSA, 1), F32), pltpu.VMEM((H_DSA, 1), F32),
                        pltpu.VMEM((H_DSA, D_HEAD, page), F32)],
    )
    return pl.pallas_call(
        functools.partial(_dsa_dec_kernel, pp=pp, n_pages=n_pages),
        grid_spec=grid_spec,
        out_shape=jax.ShapeDtypeStruct((bd, H_DSA, D_HEAD, 1), F32),
        compiler_params=_cparams(("arbitrary", "arbitrary")),
        name="dsa_decode",
    )(page_table, q, qb, k_new, v_col, selb, self_sel, *([cache_t] * pp))


def _perm_w_in(w):
    o = 0
    segs = {}
    for name, width in (("fq", W_FOX), ("fk", W_FOX), ("fv", W_FOX), ("ff", H_FOX), ("dq", W_DSA),
                        ("dk", W_DKV), ("dv", W_DKV), ("iq", W_IQ), ("ik", D_IDX), ("iw", H_IDX)):
        segs[name] = w[:, o:o + width]
        o += width
    pad = jnp.zeros((w.shape[0], LANES - D_IDX - H_FOX - H_IDX), w.dtype)
    return jnp.concatenate([segs["fq"], segs["fk"], segs["fv"], segs["dq"], segs["iq"], segs["dk"], segs["dv"],
                            segs["ik"], segs["ff"], segs["iw"], pad], axis=1)


def _heads(a, n):
    b, t, _ = a.shape
    return jnp.transpose(a.reshape(b, t, n, D_HEAD), (0, 2, 1, 3))


def _unheads(a):
    b, n, t, dh = a.shape
    return jnp.transpose(a, (0, 2, 1, 3)).reshape(b, t, n * dh)


def _augment(base, cols):
    shape = base.shape[:-1]
    cols = [jnp.broadcast_to(jnp.asarray(c, CD), shape + (1,)) for c in cols]
    pad = jnp.zeros(shape + (2 * D_HEAD - base.shape[-1] - len(cols),), CD)
    return jnp.concatenate([base.astype(CD)] + cols + [pad], axis=-1)


def _decode_operands(q, k_new, v_new, page):
    q = (q * QK_SCALE).astype(F32)
    qb = jnp.broadcast_to(q[..., None], q.shape + (page,))
    lane0 = (jnp.arange(page) == 0).astype(F32)
    return q, qb, k_new, v_new[..., None] * lane0


def _tile(n, pref):
    for t in (pref, pref // 2, pref // 4):
        if t >= 8 and n % t == 0:
            return t
    return n


def kernel(x_prompt, x_sample, c_prompt, c_sample, cache_fox_kv, cache_fox_logf, cache_dsa_kv, cache_dsa_idx,
           page_table, w_ada, b_ada, g_attn, g_ffn, w_in, b_f, qn_fox, kn_fox, qn_dsa, kn_dsa, w_out,
           peer_wq, peer_k1, peer_k2, peer_u, peer_v):
    bsz, t, d = x_prompt.shape
    bd, ts, _ = x_sample.shape
    depth = w_in.shape[0]
    n_pool, page = cache_fox_kv.shape[1], cache_fox_kv.shape[2]
    n_pages = page_table.shape[1]
    assert ts == 1 and t % LANES == 0 and t < POS_RADIX * 256 and page == LANES
    tm_p = _tile(t, TILE_ROWS)
    tm_w = _tile(t, PROJ_ROWS)
    kc = _tile(t, DSA_KC)
    pp = _tile(n_pages, DEC_PAGES) if n_pages >= 8 else n_pages

    ii = np.arange(LANES)
    u_incl = jnp.asarray(ii[:, None] <= ii[None, :], CD)
    u_strict = jnp.asarray(ii[:, None] < ii[None, :], CD)
    pg = np.arange(n_pages)
    l_strict = jnp.asarray(pg[None, :] < pg[:, None], CD)
    hh = np.arange(W_FOX) // D_HEAD
    seg = jnp.asarray(hh[:, None] == hh[None, :], CD)
    m_suf = jnp.asarray(ii[:, None] > ii[None, :], CD)

    rows = bsz + bd
    rpad = -rows % 8
    c_all = jnp.concatenate([c_prompt, c_sample, jnp.zeros((rpad, d), F32)], axis=0)
    mods = _ada(c_all, w_ada, b_ada)

    cache_fkv_t = jnp.transpose(cache_fox_kv, (0, 1, 3, 4, 5, 2))
    cache_lf_t = jnp.transpose(cache_fox_logf, (0, 1, 3, 2))
    cache_dkv_t = jnp.transpose(cache_dsa_kv, (0, 1, 3, 4, 5, 2))
    cache_idx_t = jnp.transpose(cache_dsa_idx, (0, 1, 3, 2))

    pos = np.arange(t)
    p_hi = jnp.asarray(pos // POS_RADIX, F32)
    p_lo = jnp.asarray(pos % POS_RADIX, F32)
    slopes = jnp.asarray(2.0 ** (-8.0 * (np.arange(H_DSA) + 1) / H_DSA), F32)[None, :, None, None]
    dq_cols = [slopes * POS_RADIX, slopes, -slopes * POS_RADIX * p_hi[None, None, :, None],
               -slopes * p_lo[None, None, :, None]]
    dk_cols = [p_hi[None, None, :, None], p_lo[None, None, :, None], 1.0, 1.0]
    grp = np.arange(H_DSA) // (H_DSA // KV_DSA)

    xp = x_prompt
    xs = x_sample.reshape(1, bd, d)
    outs = [[] for _ in range(8)]
    for l in range(depth):
        mp = [mods[l, :bsz, i * d:(i + 1) * d].reshape(bsz, 1, d) for i in range(6)]
        ms = [mods[l, bsz:rows, i * d:(i + 1) * d].reshape(1, bd, d) for i in range(6)]
        w_perm = _perm_w_in(w_in[l]).astype(CD)
        bias_row = jnp.zeros((1, LANES), F32).at[0, M_FF:M_FF + H_FOX].set(b_f[l])
        gains = jnp.stack([jnp.tile(qn_fox[l], H_FOX), jnp.tile(kn_fox[l], H_FOX),
                           jnp.tile(qn_dsa[l], H_DSA), jnp.tile(kn_dsa[l], H_DSA)])
        g_a = g_attn[l].reshape(1, d)
        g_f = g_ffn[l].reshape(1, d)
        w_o = w_out[l].astype(CD)

        fkv, dkv, misc, att, _ = _proj(xp, mp[0], mp[1], g_a, w_perm, bias_row, gains, seg, tm_w)
        logf = misc[..., M_FF:M_FF + H_FOX]
        c1, c2, c3 = (c[..., None] for c in _cumsum(jnp.transpose(logf, (0, 2, 1)), u_incl))
        fq_aug = _augment(_heads(att[..., C_FQ:C_FQ + W_FOX], H_FOX) * QK_SCALE, [-1.0, -1.0, -1.0, c1, c2, c3])
        fk_aug = _augment(_heads(att[..., C_FK:C_FK + W_FOX], H_FOX), [c1, c2, c3, 1.0, 1.0, 1.0])
        o_f = _fox_prompt(fq_aug, fk_aug, _heads(att[..., C_FV:C_FV + W_FOX], H_FOX), tm_p)
        dq_aug = _augment(_heads(att[..., C_DQ:C_DQ + W_DSA], H_DSA) * QK_SCALE, dq_cols)
        dk_aug = _augment(_heads(att[..., C_DK:C_DK + W_DKV], KV_DSA), dk_cols)
        o_d = _dsa_prompt(_heads(att[..., C_IQ:C_IQ + W_IQ], H_IDX), misc[..., M_IW:M_IW + H_IDX],
                          att[..., C_MISC + M_IK:C_MISC + M_IK + D_IDX], dq_aug, dk_aug,
                          _heads(att[..., C_DV:C_DV + W_DKV], KV_DSA), u_strict, min(DSA_TQ, kc), kc)
        xp = _merge(_unheads(o_f), _unheads(o_d), w_o, mp[2], xp, tm_p)
        outs[0].append(fkv.reshape(bsz, t, 2, H_FOX, D_HEAD))
        outs[1].append(logf)
        outs[2].append(dkv.reshape(bsz, t, 2, KV_DSA, D_HEAD))
        outs[3].append(misc[..., M_IK:M_IK + D_IDX])

        fkv_s, dkv_s, misc_s, _, q32_s = _proj(xs, ms[0], ms[1], g_a, w_perm, bias_row, gains, seg, bd)
        logf_s = misc_s[0, :, M_FF:M_FF + H_FOX]
        fkv_new = fkv_s.reshape(bd, 2, H_FOX, D_HEAD)
        o_fs = _fox_decode(page_table, *_decode_operands(q32_s[0, :, 0:W_FOX].reshape(bd, H_FOX, D_HEAD),
                                                         fkv_new[:, 0], fkv_new[:, 1], page),
                           logf_s[:, :, None], cache_fkv_t, cache_lf_t, m_suf, l, pp)
        selb, self_sel = _dsa_select(page_table, q32_s[0, :, W_FOX + W_DSA:].reshape(bd, H_IDX, D_IDX),
                                     misc_s[0, :, M_IW:M_IW + H_IDX][:, :, None],
                                     misc_s[0, :, None, M_IK:M_IK + D_IDX], cache_idx_t, u_strict, l_strict, l, pp)
        dkv_new = dkv_s.reshape(bd, 2, KV_DSA, D_HEAD)[:, :, grp]
        o_ds = _dsa_decode(page_table, *_decode_operands(q32_s[0, :, W_FOX:W_FOX + W_DSA].reshape(bd, H_DSA, D_HEAD),
                                                         dkv_new[:, 0], dkv_new[:, 1], page),
                           selb, self_sel, cache_dkv_t, l, pp)
        xs = _merge(o_fs.reshape(1, bd, W_FOX), o_ds.reshape(1, bd, W_DSA), w_o, ms[2], xs, bd)
        outs[4].append(fkv_s.reshape(bd, 1, 2, H_FOX, D_HEAD))
        outs[5].append(logf_s.reshape(bd, 1, H_FOX))
        outs[6].append(dkv_s.reshape(bd, 1, 2, KV_DSA, D_HEAD))
        outs[7].append(misc_s[0, :, M_IK:M_IK + D_IDX].reshape(bd, 1, D_IDX))

        wq_t = jnp.transpose(peer_wq[l]).astype(CD)
        k1p = jnp.pad(peer_k1[l], ((0, 0), (0, 0), (0, D_KEY // 2)))
        k2p = jnp.pad(peer_k2[l], ((0, 0), (0, 0), (D_KEY // 2, 0)))
        kh = jnp.concatenate([k1p, k2p], axis=1)
        eye = jnp.eye(PEER_HEADS, dtype=kh.dtype)
        kbd_t = (kh[:, :, None, :] * eye[:, None, :, None]).reshape(PEER_HEADS * 2 * N_KEYS,
                                                                    PEER_HEADS * D_KEY).astype(CD)
        u_c = peer_u[l].astype(CD)
        v_c = peer_v[l].astype(CD)
        xp = _peer_b(*_peer_a(xp, mp[3], mp[4], g_f, wq_t, kbd_t, tm_w), u_c, v_c, xp, mp[5], tm_p, PEER_IC)
        xs = _peer_b(*_peer_a(xs, ms[3], ms[4], g_f, jnp.transpose(peer_wq[l]), kh, bd),
                     u_c, v_c, xs, ms[5], bd, PEER_IC)

    stk = [jnp.stack(o) for o in outs]
    return (xp, xs.reshape(bd, 1, d), stk[0], stk[1], stk[2], stk[3], stk[4], stk[5], stk[6], stk[7])
```
